```python
import jax
import jax.numpy as jnp
from jax import lax
import numpy as np

D_MODEL = 1024
BATCH = 8
SEQ = 8192
DEPTH = 2
DEC_BATCH = 2
DEC_SEQ = 16384
PAST_LEN = 128

EPS = 1e-6
LOG_FLOOR = 1e-30
MIX_W = 512
N_MEM = 256
GLA_HEADS = 4
GLA_DK = 64
GLA_DV = 128
GLA_GATE_RANK = 16
GLA_GATE_NORM = 16.0
HGRN_HEADS = 4
HGRN_DK = 128
HGRN_DV = 128
RWKV_HEADS = 8
RWKV_HEAD = 64
RWKV_DECAY_RANK = 64
RWKV_AAA_RANK = 64
RWKV_GATE_RANK = 128
RWKV_DECAY_SCALE = 0.606531
RWKV_GN_EPS = 64e-5
CHUNK = 32
X_HEADS = 4
X_HEAD = D_MODEL // X_HEADS
PEER_HEADS = 8
PEER_DK = 128
PEER_NKEYS = 128
PEER_EXPERTS = PEER_NKEYS * PEER_NKEYS
PEER_TOPK = 16
PEER_BLOCK = 128

GLA_COLS = (GLA_HEADS * GLA_DK, GLA_HEADS * GLA_DK, GLA_HEADS * GLA_DV, GLA_HEADS * GLA_DV, GLA_GATE_RANK, GLA_GATE_RANK)
HGRN_COLS = (HGRN_HEADS * HGRN_DK, HGRN_HEADS * HGRN_DK, HGRN_HEADS * HGRN_DK, HGRN_HEADS * HGRN_DV, HGRN_HEADS * HGRN_DV)
RWKV_COLS = (MIX_W, MIX_W, MIX_W, RWKV_DECAY_RANK, RWKV_DECAY_RANK, RWKV_AAA_RANK, RWKV_GATE_RANK)
GLA_IN = sum(GLA_COLS)
HGRN_IN = sum(HGRN_COLS)
RWKV_IN = sum(RWKV_COLS)
GATE_OFF = GLA_IN + HGRN_IN + RWKV_IN
P_IN = GATE_OFF + 3 * D_MODEL

kernel_name = 'hybrid_bidir_gla_hgrn2_rwkv7_peer_encoder'


def split_cols(z, sizes):
    out, off = [], 0
    for s in sizes:
        out.append(z[..., off:off + s])
        off += s
    return out


def rms_norm(x, g):
    xf = x.astype(jnp.float32)
    y = xf * lax.rsqrt(jnp.mean(xf * xf, axis=-1, keepdims=True) + EPS)
    return (y * g.astype(jnp.float32)).astype(x.dtype)


def head_rms_norm(o, g):
    of = o.astype(jnp.float32)
    y = of * lax.rsqrt(jnp.mean(of * of, axis=-1, keepdims=True) + EPS)
    return y.reshape(o.shape[:-2] + (-1,)) * g.astype(jnp.float32)


def to_chunks(z):
    b, t, h, d = z.shape
    return z.reshape(b, t // CHUNK, CHUNK, h, d).transpose(1, 0, 3, 2, 4)


def from_chunks(z):
    nc, b, h, c, d = z.shape
    return z.transpose(1, 0, 3, 2, 4).reshape(b, nc * c, h, d)


def gated_linear_scan(qc, kc, vc, gc, reverse):
    nc, b, h, c, dk = qc.shape
    dv = vc.shape[-1]
    incl = jnp.tril(jnp.ones((c, c), dtype=bool))[:, :, None]

    def step(state, inp):
        q, k, v, g = (z.astype(jnp.float32) for z in inp)
        if reverse:
            q, k, v, g = (z[..., ::-1, :] for z in (q, k, v, g))
        cum = jnp.cumsum(g, axis=-2)
        o_inter = jnp.einsum('bhck,bhkv->bhcv', q * jnp.exp(cum), state)
        rel = cum[..., :, None, :] - cum[..., None, :, :]
        decay = jnp.where(incl, jnp.exp(jnp.where(incl, rel, 0.0)), 0.0)
        scores = jnp.einsum('bhik,bhjk,bhijk->bhij', q, k, decay)
        o = o_inter + jnp.einsum('bhij,bhjv->bhiv', scores, v)
        last = cum[..., -1:, :]
        state = (jnp.exp(last[..., 0, :])[..., None] * state
                 + jnp.einsum('bhck,bhcv->bhkv', k * jnp.exp(last - cum), v))
        if reverse:
            o = o[..., ::-1, :]
        return state, o

    s0 = jnp.zeros((b, h, dk, dv), jnp.float32)
    _, out = lax.scan(step, s0, (qc, kc, vc, gc), reverse=reverse)
    return out


def bidirectional_gla(q, k_f, k_b, v, g_f, g_b):
    qc, vc = to_chunks(q), to_chunks(v)
    o = (gated_linear_scan(qc, to_chunks(k_f), vc, to_chunks(g_f), False)
         + gated_linear_scan(qc, to_chunks(k_b), vc, to_chunks(g_b), True))
    return from_chunks(o)


def gla_branch(h, w_cols, up_f, up_b, bias_f, bias_b, norm_g):
    b, t, _ = h.shape
    q, k, v, og, gd_f, gd_b = split_cols(h @ w_cols, GLA_COLS)
    heads = lambda z: z.reshape(b, t, GLA_HEADS, -1)
    g_f = jax.nn.log_sigmoid((gd_f @ up_f + bias_f).astype(jnp.float32)) / GLA_GATE_NORM
    g_b = jax.nn.log_sigmoid((gd_b @ up_b + bias_b).astype(jnp.float32)) / GLA_GATE_NORM
    kh = heads(k)
    o = bidirectional_gla(heads(q) * GLA_DK ** -0.5, kh, kh, heads(v), heads(g_f), heads(g_b))
    return (head_rms_norm(o, norm_g) * jax.nn.silu(og.astype(jnp.float32))).astype(h.dtype)


def hgrn_lower_bounds(logits):
    sm = jax.nn.softmax(logits.astype(jnp.float32), axis=0)
    return jnp.cumsum(sm, axis=0) - sm[0]


def hgrn2_branch(h, w_cols, lb, norm_g):
    b, t, _ = h.shape
    q, zf_f, zf_b, i, og = split_cols(h @ w_cols, HGRN_COLS)
    heads = lambda z: z.reshape(b, t, HGRN_HEADS, -1)
    lb = lb.astype(jnp.float32)

    def forget(z):
        z = z.astype(jnp.float32)
        f = lb + (1.0 - lb) * jax.nn.sigmoid(z)
        log_f = jnp.log(jnp.maximum(f, LOG_FLOOR))
        return heads((1.0 - lb) * jax.nn.sigmoid(-z)), heads(log_f)

    k_f, g_f = forget(zf_f)
    k_b, g_b = forget(zf_b)
    o = bidirectional_gla(heads(jax.nn.silu(q)), k_f, k_b, heads(i), g_f, g_b)
    return (head_rms_norm(o, norm_g) * jax.nn.silu(og.astype(jnp.float32))).astype(h.dtype)


def rwkv7_scan(r, w, k, v, a, bvec, reverse):
    t, b, h, n = r.shape

    def step(state, inp):
        r_t, w_t, k_t, v_t, a_t, b_t = inp
        sa = jnp.einsum('bhvk,bhk->bhv', state, a_t)
        state = (state * w_t[:, :, None, :] + sa[..., None] * b_t[:, :, None, :]
                 + v_t[..., None] * k_t[:, :, None, :])
        return state, jnp.einsum('bhvk,bhk->bhv', state, r_t)

    s0 = jnp.zeros((b, h, n, n), jnp.float32)
    _, out = lax.scan(step, s0, (r, w, k, v, a, bvec), reverse=reverse)
    return out


def rwkv7_branch(h, w_cols, mu_f, mu_b, w0_f, w2_f, w0_b, w2_b, a0, a2, g2, k_k, k_a, r_k, norm_g, norm_b):
    f32 = jnp.float32
    p = jnp.einsum('btd,dc->tbc', h, w_cols)
    prev = jnp.concatenate([jnp.zeros_like(p[:1]), p[:-1]], axis=0)
    nxt = jnp.concatenate([p[1:], jnp.zeros_like(p[:1])], axis=0)
    p = p + mu_f * (prev - p) + mu_b * (nxt - p)
    r, k, v, wd_f, wd_b, ad, gd = split_cols(p, RWKV_COLS)
    t, b, _ = p.shape
    heads = lambda z: z.reshape(t, b, RWKV_HEADS, RWKV_HEAD)
    w_f = jnp.exp(-RWKV_DECAY_SCALE * jax.nn.sigmoid((w0_f + jnp.tanh(wd_f) @ w2_f).astype(f32)))
    w_b = jnp.exp(-RWKV_DECAY_SCALE * jax.nn.sigmoid((w0_b + jnp.tanh(wd_b) @ w2_b).astype(f32)))
    a = jax.nn.sigmoid((a0 + ad @ a2).astype(f32))
    g = (jax.nn.sigmoid(gd) @ g2).astype(f32)
    kf = k.astype(f32)
    kk = heads(kf * k_k)
    kk = kk / jnp.maximum(jnp.sqrt(jnp.sum(kk * kk, axis=-1, keepdims=True)), 1e-12)
    kh = heads(kf * (1.0 + (a - 1.0) * k_a))
    rh, vh, ah = heads(r.astype(f32)), heads(v.astype(f32)), heads(a)
    o = (rwkv7_scan(rh, heads(w_f), kh, vh, -kk, kk * ah, False)
         + rwkv7_scan(rh, heads(w_b), kh, vh, -kk, kk * ah, True))
    mean = jnp.mean(o, axis=-1, keepdims=True)
    var = jnp.mean(jnp.square(o - mean), axis=-1, keepdims=True)
    on = ((o - mean) * lax.rsqrt(var + RWKV_GN_EPS)).reshape(t, b, -1) * norm_g + norm_b
    bonus = (jnp.sum(rh * kh * r_k.reshape(RWKV_HEADS, RWKV_HEAD), axis=-1, keepdims=True) * vh).reshape(t, b, -1)
    return ((on + bonus) * g).astype(h.dtype)


def memory_cross_attention(h, mem, g_mem, wq, wk, wv, wo):
    b, t, d = h.shape
    m = rms_norm(mem, g_mem)
    n_mem = mem.shape[1]
    q = (h @ wq).reshape(b, t, X_HEADS, X_HEAD)
    k = (m @ wk).reshape(b, n_mem, X_HEADS, X_HEAD)
    v = (m @ wv).reshape(b, n_mem, X_HEADS, X_HEAD)
    s = jnp.einsum('bthd,bmhd->bhtm', q, k).astype(jnp.float32) * X_HEAD ** -0.5
    pr = jax.nn.softmax(s, axis=-1).astype(h.dtype)
    o = jnp.einsum('bhtm,bmhd->bthd', pr, v).reshape(b, t, d)
    return o @ wo


def peer_ffn(h, wq, sk1, sk2, u, v):
    b, t, d = h.shape
    blocks = h.reshape(b * t // PEER_BLOCK, PEER_BLOCK, d)
    half = PEER_DK // 2

    def block(xb):
        q = (xb @ wq).reshape(PEER_BLOCK, PEER_HEADS, PEER_DK)
        s1 = jnp.einsum('thd,nd->thn', q[..., :half], sk1)
        s2 = jnp.einsum('thd,nd->thn', q[..., half:], sk2)
        v1, i1 = lax.top_k(s1, PEER_TOPK)
        v2, i2 = lax.top_k(s2, PEER_TOPK)
        cand = (v1[..., :, None] + v2[..., None, :]).reshape(PEER_BLOCK, PEER_HEADS, PEER_TOPK * PEER_TOPK)
        cidx = (i1[..., :, None] * PEER_NKEYS + i2[..., None, :]).reshape(PEER_BLOCK, PEER_HEADS, PEER_TOPK * PEER_TOPK)
        top, pos = lax.top_k(cand, PEER_TOPK)
        idx = jnp.take_along_axis(cidx, pos, axis=-1)
        gate = jax.nn.softmax(top.astype(jnp.float32), axis=-1)
        act = jax.nn.gelu(jnp.einsum('td,thkd->thk', xb, u[idx]).astype(jnp.float32), approximate=False)
        return jnp.einsum('thk,thkd->td', (gate * act).astype(xb.dtype), v[idx])

    return lax.map(block, blocks).reshape(b, t, d)


def encoder_layer(x, mem, l, lower_bound, P):
    h = rms_norm(x, P['norm_mix_g'][l])
    w_in = P['w_in'][l]
    o_gla = gla_branch(h, w_in[:, :GLA_IN], P['gla_gate_up_f'][l], P['gla_gate_up_b'][l],
                       P['gla_gate_bias_f'][l], P['gla_gate_bias_b'][l], P['gla_norm_g'][l])
    o_hgrn = hgrn2_branch(h, w_in[:, GLA_IN:GLA_IN + HGRN_IN], lower_bound, P['hgrn_norm_g'][l])
    o_rwkv = rwkv7_branch(h, w_in[:, GLA_IN + HGRN_IN:GATE_OFF], P['rwkv_mu_f'][l], P['rwkv_mu_b'][l],
                          P['rwkv_w0_f'][l], P['rwkv_w2_f'][l], P['rwkv_w0_b'][l], P['rwkv_w2_b'][l],
                          P['rwkv_a0'][l], P['rwkv_a2'][l], P['rwkv_g2'][l], P['rwkv_k_k'][l],
                          P['rwkv_k_a'][l], P['rwkv_r_k'][l], P['rwkv_norm_g'][l], P['rwkv_norm_b'][l])
    gate = lambda i: jax.nn.sigmoid(h @ w_in[:, GATE_OFF + i * D_MODEL:GATE_OFF + (i + 1) * D_MODEL])
    merged = (gate(0) * (o_gla @ P['w_branch_gla'][l])
              + gate(1) * (o_hgrn @ P['w_branch_hgrn'][l])
              + gate(2) * jnp.einsum('tbc,cd->btd', o_rwkv, P['w_branch_rwkv'][l]))
    x = x + merged @ P['w_out'][l]
    x = x + memory_cross_attention(rms_norm(x, P['norm_x_g'][l]), mem, P['norm_mem_g'][l],
                                   P['xattn_wq'][l], P['xattn_wk'][l], P['xattn_wv'][l], P['xattn_wo'][l])
    x = x + peer_ffn(rms_norm(x, P['norm_ffn_g'][l]), P['peer_wq'][l], P['peer_subkeys_1'][l],
                     P['peer_subkeys_2'][l], P['peer_u'][l], P['peer_v'][l])
    return x


def run_trunk(x, mem, P):
    lbs = hgrn_lower_bounds(P['hgrn_lb_logits'])
    for l in range(DEPTH):
        x = encoder_layer(x, mem, l, lbs[l], P)
    return rms_norm(x, P['final_norm_g'])


def setup_inputs(seed: int = 0) -> dict:
    key = jax.random.key(seed)
    ks = iter(jax.random.split(key, 64))
    nrm = lambda shape, scale: scale * jax.random.normal(next(ks), shape, jnp.float32)
    gain = lambda shape: 1.0 + nrm(shape, 0.02)
    uni = lambda shape: jax.random.uniform(next(ks), shape, jnp.float32, 0.0, 0.5)
    L, D = DEPTH, D_MODEL
    return {
        'x_prompt': nrm((BATCH, SEQ, D), 1.0),
        'x_sample': nrm((DEC_BATCH, DEC_SEQ, D), 1.0),
        'mem_prompt': nrm((BATCH, N_MEM, D), 1.0),
        'mem_sample': nrm((DEC_BATCH, N_MEM, D), 1.0),
        'norm_mix_g': gain((L, D)),
        'w_in': nrm((L, D, P_IN), D ** -0.5),
        'gla_gate_up_f': nrm((L, GLA_GATE_RANK, GLA_HEADS * GLA_DK), GLA_GATE_RANK ** -0.5),
        'gla_gate_up_b': nrm((L, GLA_GATE_RANK, GLA_HEADS * GLA_DK), GLA_GATE_RANK ** -0.5),
        'gla_gate_bias_f': nrm((L, GLA_HEADS * GLA_DK), 0.1),
        'gla_gate_bias_b': nrm((L, GLA_HEADS * GLA_DK), 0.1),
        'gla_norm_g': gain((L, MIX_W)),
        'hgrn_lb_logits': nrm((L, HGRN_HEADS * HGRN_DK), 1.0),
        'hgrn_norm_g': gain((L, MIX_W)),
        'rwkv_mu_f': uni((L, RWKV_IN)),
        'rwkv_mu_b': uni((L, RWKV_IN)),
        'rwkv_w0_f': nrm((L, MIX_W), 1.0),
        'rwkv_w2_f': nrm((L, RWKV_DECAY_RANK, MIX_W), 0.1),
        'rwkv_w0_b': nrm((L, MIX_W), 1.0),
        'rwkv_w2_b': nrm((L, RWKV_DECAY_RANK, MIX_W), 0.1),
        'rwkv_a0': nrm((L, MIX_W), 0.5),
        'rwkv_a2': nrm((L, RWKV_AAA_RANK, MIX_W), 0.1),
        'rwkv_g2': nrm((L, RWKV_GATE_RANK, MIX_W), RWKV_GATE_RANK ** -0.5),
        'rwkv_k_k': 0.85 + nrm((L, MIX_W), 0.02),
        'rwkv_k_a': gain((L, MIX_W)),
        'rwkv_r_k': nrm((L, MIX_W), 0.1),
        'rwkv_norm_g': gain((L, MIX_W)),
        'rwkv_norm_b': nrm((L, MIX_W), 0.02),
        'w_branch_gla': nrm((L, MIX_W, D), MIX_W ** -0.5),
        'w_branch_hgrn': nrm((L, MIX_W, D), MIX_W ** -0.5),
        'w_branch_rwkv': nrm((L, MIX_W, D), MIX_W ** -0.5),
        'w_out': nrm((L, D, D), D ** -0.5),
        'norm_x_g': gain((L, D)),
        'norm_mem_g': gain((L, D)),
        'xattn_wq': nrm((L, D, D), D ** -0.5),
        'xattn_wk': nrm((L, D, D), D ** -0.5),
        'xattn_wv': nrm((L, D, D), D ** -0.5),
        'xattn_wo': nrm((L, D, D), D ** -0.5),
        'norm_ffn_g': gain((L, D)),
        'peer_wq': nrm((L, D, PEER_HEADS * PEER_DK), D ** -0.5),
        'peer_subkeys_1': nrm((L, PEER_NKEYS, PEER_DK // 2), (PEER_DK // 2) ** -0.5),
        'peer_subkeys_2': nrm((L, PEER_NKEYS, PEER_DK // 2), (PEER_DK // 2) ** -0.5),
        'peer_u': nrm((L, PEER_EXPERTS, D), D ** -0.5),
        'peer_v': nrm((L, PEER_EXPERTS, D), 0.3),
        'final_norm_g': gain((D,)),
    }


def reference(x_prompt, x_sample, mem_prompt, mem_sample, norm_mix_g, w_in, gla_gate_up_f, gla_gate_up_b,
              gla_gate_bias_f, gla_gate_bias_b, gla_norm_g, hgrn_lb_logits, hgrn_norm_g, rwkv_mu_f, rwkv_mu_b,
              rwkv_w0_f, rwkv_w2_f, rwkv_w0_b, rwkv_w2_b, rwkv_a0, rwkv_a2, rwkv_g2, rwkv_k_k, rwkv_k_a,
              rwkv_r_k, rwkv_norm_g, rwkv_norm_b, w_branch_gla, w_branch_hgrn, w_branch_rwkv, w_out,
              norm_x_g, norm_mem_g, xattn_wq, xattn_wk, xattn_wv, xattn_wo, norm_ffn_g, peer_wq,
              peer_subkeys_1, peer_subkeys_2, peer_u, peer_v, final_norm_g):
    P = dict(norm_mix_g=norm_mix_g, w_in=w_in, gla_gate_up_f=gla_gate_up_f, gla_gate_up_b=gla_gate_up_b,
             gla_gate_bias_f=gla_gate_bias_f, gla_gate_bias_b=gla_gate_bias_b, gla_norm_g=gla_norm_g,
             hgrn_lb_logits=hgrn_lb_logits, hgrn_norm_g=hgrn_norm_g, rwkv_mu_f=rwkv_mu_f, rwkv_mu_b=rwkv_mu_b,
             rwkv_w0_f=rwkv_w0_f, rwkv_w2_f=rwkv_w2_f, rwkv_w0_b=rwkv_w0_b, rwkv_w2_b=rwkv_w2_b,
             rwkv_a0=rwkv_a0, rwkv_a2=rwkv_a2, rwkv_g2=rwkv_g2, rwkv_k_k=rwkv_k_k, rwkv_k_a=rwkv_k_a,
             rwkv_r_k=rwkv_r_k, rwkv_norm_g=rwkv_norm_g, rwkv_norm_b=rwkv_norm_b, w_branch_gla=w_branch_gla,
             w_branch_hgrn=w_branch_hgrn, w_branch_rwkv=w_branch_rwkv, w_out=w_out, norm_x_g=norm_x_g,
             norm_mem_g=norm_mem_g, xattn_wq=xattn_wq, xattn_wk=xattn_wk, xattn_wv=xattn_wv, xattn_wo=xattn_wo,
             norm_ffn_g=norm_ffn_g, peer_wq=peer_wq, peer_subkeys_1=peer_subkeys_1,
             peer_subkeys_2=peer_subkeys_2, peer_u=peer_u, peer_v=peer_v, final_norm_g=final_norm_g)
    y_prompt = run_trunk(x_prompt, mem_prompt, P)
    y_sample = run_trunk(x_sample, mem_sample, P)
    return (y_prompt, y_sample)
```

```python
import functools

import jax
import jax.numpy as jnp
from jax import lax
from jax.experimental import pallas as pl
from jax.experimental.pallas import tpu as pltpu

F32 = jnp.float32
BF16 = jnp.bfloat16

D_MODEL = 1024
EPS = 1e-6
LOG_FLOOR = 1e-30
MIX_W = 512
GLA_HEADS, GLA_DK, GLA_DV = 4, 64, 128
GLA_GATE_RANK = 16
GLA_GATE_NORM = 16.0
HGRN_HEADS, HGRN_DK, HGRN_DV = 4, 128, 128
RWKV_HEADS, RWKV_HEAD = 8, 64
RWKV_DECAY_RANK, RWKV_AAA_RANK, RWKV_GATE_RANK = 64, 64, 128
RWKV_DECAY_SCALE = 0.606531
RWKV_GN_EPS = 64e-5
X_HEADS = 4
X_HEAD = D_MODEL // X_HEADS
PEER_HEADS, PEER_DK, PEER_NKEYS, PEER_TOPK = 8, 128, 128, 16

GLA_W = GLA_HEADS * GLA_DK
GLA_IN = 2 * GLA_W + 2 * MIX_W + 2 * GLA_GATE_RANK
HGRN_W = HGRN_HEADS * HGRN_DK
HGRN_IN = 5 * MIX_W
RWKV_IN = 3 * MIX_W + 2 * RWKV_DECAY_RANK + RWKV_AAA_RANK + RWKV_GATE_RANK
GATE_OFF = GLA_IN + HGRN_IN + RWKV_IN

LANE = 128
ZGLA_W = 2 * GLA_W + 2 * MIX_W + LANE
ZRWKV_W = 3 * MIX_W + 4 * LANE

ROW_TILE = 512
MERGE_TILE = 256
GLA_CHUNK = 32
GLA_TB = 128
RWKV_CHUNK = 64
RWKV_TB = 128
PREP_TB = 256
XATTN_TQ = 256
TOPK_TB = 128
PEER_TE = 8
VMEM_LIMIT = 48 * 1024 * 1024


def _params(sem):
    return pltpu.CompilerParams(dimension_semantics=sem, vmem_limit_bytes=VMEM_LIMIT)


def _dot(a, b):
    return jnp.dot(a.astype(BF16), b.astype(BF16), preferred_element_type=F32)


def _dot_nt(a, b):
    return lax.dot_general(a.astype(BF16), b.astype(BF16), (((1,), (1,)), ((), ())),
                           preferred_element_type=F32)


def _dot_nt_f32(a, b):
    return lax.dot_general(a, b, (((1,), (1,)), ((), ())), precision=lax.Precision.HIGHEST,
                           preferred_element_type=F32)


def _dot_tn(a, b):
    return lax.dot_general(a.astype(BF16), b.astype(BF16), (((0,), (0,)), ((), ())),
                           preferred_element_type=F32)


def _split3(x):
    hi = x.astype(BF16)
    r1 = x - hi.astype(F32)
    mid = r1.astype(BF16)
    lo = (r1 - mid.astype(F32)).astype(BF16)
    return hi, mid, lo


def _dot_exact_rhs(x, m):
    hi, mid, lo = _split3(x)
    f = lambda p: jnp.dot(p, m, preferred_element_type=F32)
    return f(hi) + f(mid) + f(lo)


def _dot_exact_lhs(m, x):
    hi, mid, lo = _split3(x)
    f = lambda p: jnp.dot(m, p, preferred_element_type=F32)
    return f(hi) + f(mid) + f(lo)


def _tri_incl(n):
    r = lax.broadcasted_iota(jnp.int32, (n, n), 0)
    c = lax.broadcasted_iota(jnp.int32, (n, n), 1)
    return (r >= c).astype(BF16)


def _rmsnorm(x, g):
    return x * lax.rsqrt(jnp.mean(x * x, axis=-1, keepdims=True) + EPS) * g


def _log_sigmoid(x):
    return jnp.minimum(x, 0.0) - jnp.log(1.0 + jnp.exp(-jnp.abs(x)))


def _silu(x):
    return x * jax.nn.sigmoid(x)


def _norm_matmul_kernel(x_ref, g_ref, w_ref, o_ref, *h_ref):
    h = _rmsnorm(x_ref[...], g_ref[...])
    if w_ref.dtype == F32:
        o_ref[...] = jnp.dot(h, w_ref[...], precision=lax.Precision.HIGHEST, preferred_element_type=F32)
    else:
        o_ref[...] = jnp.dot(h.astype(BF16), w_ref[...], preferred_element_type=F32)
    if h_ref:
        h_ref[0][...] = h


def _norm_matmul(x, g, w, emit_h=False):
    n, d = x.shape
    c = w.shape[1]
    tm = min(ROW_TILE, n)
    out_shape = [jax.ShapeDtypeStruct((n, c), F32)]
    out_specs = [pl.BlockSpec((tm, c), lambda i: (i, 0))]
    if emit_h:
        out_shape.append(jax.ShapeDtypeStruct((n, d), F32))
        out_specs.append(pl.BlockSpec((tm, d), lambda i: (i, 0)))
    res = pl.pallas_call(
        _norm_matmul_kernel,
        grid=(n // tm,),
        in_specs=[pl.BlockSpec((tm, d), lambda i: (i, 0)),
                  pl.BlockSpec((1, d), lambda i: (0, 0)),
                  pl.BlockSpec((d, c), lambda i: (0, 0))],
        out_specs=out_specs,
        out_shape=out_shape,
        compiler_params=_params(("parallel",)),
    )(x, g.reshape(1, d), w)
    return res if emit_h else res[0]


def _final_norm_kernel(x_ref, g_ref, o_ref):
    o_ref[...] = _rmsnorm(x_ref[...], g_ref[...])


def _final_norm(x, g):
    n, d = x.shape
    tm = min(ROW_TILE, n)
    return pl.pallas_call(
        _final_norm_kernel,
        grid=(n // tm,),
        in_specs=[pl.BlockSpec((tm, d), lambda i: (i, 0)), pl.BlockSpec((1, d), lambda i: (0, 0))],
        out_specs=pl.BlockSpec((tm, d), lambda i: (i, 0)),
        out_shape=jax.ShapeDtypeStruct((n, d), F32),
        compiler_params=_params(("parallel",)),
    )(x, g.reshape(1, d))


def _gla_time_block(q, k, v, g, o_scr, st_scr, qs, ks, vs, cs, ebc_ref, bdt_ref, reverse):
    tb = q.shape[0]
    nchunks = tb // GLA_CHUNK
    tri = _tri_incl(GLA_CHUNK)
    rows = lax.broadcasted_iota(jnp.int32, (GLA_CHUNK, 1), 0)
    order = range(nchunks - 1, -1, -1) if reverse else range(nchunks)
    for c in order:
        sl = slice(c * GLA_CHUNK, (c + 1) * GLA_CHUNK)
        qc, kc, vc, gc = q[sl], k[sl], v[sl], g[sl]
        cum = _dot_exact_lhs(tri, gc)
        tot = cum[GLA_CHUNK - 1:GLA_CHUNK]
        if reverse:
            cum = tot - cum + gc
        st = st_scr[...]
        o = _dot_nt(qc * jnp.exp(cum), st)
        qs[...] = qc
        ks[...] = kc
        vs[...] = vc
        cs[...] = cum

        def body(j, o):
            cj = cs[pl.ds(j, 1), :]
            kj = ks[pl.ds(j, 1), :]
            vj = vs[pl.ds(j, 1), :]
            dec = jnp.exp(jnp.minimum(cs[...] - cj, 0.0))
            live = (rows <= j) if reverse else (rows >= j)
            a = jnp.where(live, qs[...] * kj * dec, 0.0)
            return o + jnp.dot(a.astype(BF16), ebc_ref[...], preferred_element_type=F32) * vj

        o = lax.fori_loop(0, GLA_CHUNK, body, o)
        o_scr[sl, :] = o
        kd = kc * jnp.exp(tot - cum)
        st_scr[...] = st * jnp.exp(tot) + _dot_tn(vc, kd) * bdt_ref[...]


def _head_rms_gate(o, og, hh_ref, ng_ref, dv):
    ms = _dot_exact_rhs(o * o, hh_ref[...]) * (1.0 / dv)
    return o * lax.rsqrt(ms + EPS) * ng_ref[...] * _silu(og)


def _gla_kernel(*refs, reverse, final):
    if final:
        (z_ref, up_ref, bias_ref, ebc_ref, bdt_ref, oprev_ref, hh_ref, ng_ref,
         o_ref, st_scr, o_scr, qs, ks, vs, cs) = refs
    else:
        z_ref, up_ref, bias_ref, ebc_ref, bdt_ref, o_ref, st_scr, o_scr, qs, ks, vs, cs = refs

    @pl.when(pl.program_id(1) == 0)
    def _():
        st_scr[...] = jnp.zeros_like(st_scr)

    z = z_ref[...]
    q = z[:, 0:GLA_W] * (GLA_DK ** -0.5)
    k = z[:, GLA_W:2 * GLA_W]
    v = z[:, 2 * GLA_W:2 * GLA_W + MIX_W]
    gd = z[:, 2 * GLA_W + 2 * MIX_W:ZGLA_W]
    g = _log_sigmoid(_dot(gd, up_ref[...]) + bias_ref[...]) * (1.0 / GLA_GATE_NORM)
    _gla_time_block(q, k, v, g, o_scr, st_scr, qs, ks, vs, cs, ebc_ref, bdt_ref, reverse)
    if final:
        og = z[:, 2 * GLA_W + MIX_W:2 * GLA_W + 2 * MIX_W]
        o_ref[...] = _head_rms_gate(oprev_ref[...] + o_scr[...], og, hh_ref, ng_ref, GLA_DV)
    else:
        o_ref[...] = o_scr[...]


def _hgrn_kernel(*refs, reverse, final):
    if final:
        (z_ref, lb_ref, ebc_ref, bdt_ref, oprev_ref, hh_ref, ng_ref,
         o_ref, st_scr, o_scr, qs, ks, vs, cs) = refs
    else:
        z_ref, lb_ref, ebc_ref, bdt_ref, o_ref, st_scr, o_scr, qs, ks, vs, cs = refs

    @pl.when(pl.program_id(1) == 0)
    def _():
        st_scr[...] = jnp.zeros_like(st_scr)

    z = z_ref[...]
    lb = lb_ref[...]
    q = _silu(z[:, 0:HGRN_W])
    zf = z[:, (2 if reverse else 1) * HGRN_W:(3 if reverse else 2) * HGRN_W]
    v = z[:, 3 * HGRN_W:4 * HGRN_W]
    f = lb + (1.0 - lb) * jax.nn.sigmoid(zf)
    g = jnp.log(jnp.maximum(f, LOG_FLOOR))
    k = (1.0 - lb) * jax.nn.sigmoid(-zf)
    _gla_time_block(q, k, v, g, o_scr, st_scr, qs, ks, vs, cs, ebc_ref, bdt_ref, reverse)
    if final:
        og = z[:, 4 * HGRN_W:5 * HGRN_W]
        o_ref[...] = _head_rms_gate(oprev_ref[...] + o_scr[...], og, hh_ref, ng_ref, HGRN_DV)
    else:
        o_ref[...] = o_scr[...]


def _seq_row_map(nblk, reverse):
    if reverse:
        return lambda b, i: (b * nblk + nblk - 1 - i, 0)
    return lambda b, i: (b * nblk + i, 0)


def _const_map(b, i):
    return (0, 0)


def _lin_attn_pass(kernel, z, consts, o_prev, final_consts, bsz, t, w, reverse):
    n, zc = z.shape
    tb = min(GLA_TB, t)
    nblk = t // tb
    rmap = _seq_row_map(nblk, reverse)
    final = o_prev is not None
    args = [z] + list(consts)
    in_specs = [pl.BlockSpec((tb, zc), rmap)] + [pl.BlockSpec(c.shape, _const_map) for c in consts]
    if final:
        args += [o_prev] + list(final_consts)
        in_specs += [pl.BlockSpec((tb, MIX_W), rmap)]
        in_specs += [pl.BlockSpec(c.shape, _const_map) for c in final_consts]
    return pl.pallas_call(
        functools.partial(kernel, reverse=reverse, final=final),
        grid=(bsz, nblk),
        in_specs=in_specs,
        out_specs=pl.BlockSpec((tb, MIX_W), rmap),
        out_shape=jax.ShapeDtypeStruct((n, MIX_W), F32),
        scratch_shapes=[pltpu.VMEM((MIX_W, w), F32), pltpu.VMEM((tb, MIX_W), F32),
                        pltpu.VMEM((GLA_CHUNK, w), F32), pltpu.VMEM((GLA_CHUNK, w), F32),
                        pltpu.VMEM((GLA_CHUNK, MIX_W), F32), pltpu.VMEM((GLA_CHUNK, w), F32)],
        compiler_params=_params(("parallel", "arbitrary")),
    )(*args)


def _head_match(n_rows, row_blk, n_cols, col_blk, dtype):
    r = jnp.arange(n_rows) // row_blk
    c = jnp.arange(n_cols) // col_blk
    return (r[:, None] == c[None, :]).astype(dtype)


def _rwkv_prep_kernel(z_ref, zp_ref, zn_ref, muf_ref, mub_ref, w0f_ref, w2f_ref, w0b_ref, w2b_ref,
                      a0_ref, a2_ref, g2_ref, kk_ref, ka_ref, rk_ref, hh_ref,
                      r_out, kh_out, v_out, kkn_out, b_out, lwf_out, lwb_out, g_out, bonus_out):
    i = pl.program_id(1)
    last = pl.num_programs(1) - 1
    z = z_ref[...]
    tb = z.shape[0]
    rows = lax.broadcasted_iota(jnp.int32, (tb, 1), 0)
    hp = jnp.where(i == 0, 0.0, zp_ref[7:8, :])
    hn = jnp.where(i == last, 0.0, zn_ref[0:1, :])
    prev = jnp.where(rows == 0, hp, pltpu.roll(z, 1, 0))
    nxt = jnp.where(rows == tb - 1, hn, pltpu.roll(z, tb - 1, 0))
    p = z + muf_ref[...] * (prev - z) + mub_ref[...] * (nxt - z)
    r = p[:, 0:MIX_W]
    k = p[:, MIX_W:2 * MIX_W]
    v = p[:, 2 * MIX_W:3 * MIX_W]
    o = 3 * MIX_W
    wdf, wdb, ad, gd = (p[:, o + j * LANE:o + (j + 1) * LANE] for j in range(4))
    lwf = -RWKV_DECAY_SCALE * jax.nn.sigmoid(w0f_ref[...] + _dot(jnp.tanh(wdf), w2f_ref[...]))
    lwb = -RWKV_DECAY_SCALE * jax.nn.sigmoid(w0b_ref[...] + _dot(jnp.tanh(wdb), w2b_ref[...]))
    a = jax.nn.sigmoid(a0_ref[...] + _dot(ad, a2_ref[...]))
    g = _dot(jax.nn.sigmoid(gd), g2_ref[...])
    kk = k * kk_ref[...]
    ss = _dot_exact_rhs(kk * kk, hh_ref[...])
    kk = kk / jnp.maximum(jnp.sqrt(ss), 1e-12)
    kh = k * (1.0 + (a - 1.0) * ka_ref[...])
    bonus = _dot_exact_rhs(r * kh * rk_ref[...], hh_ref[...]) * v
    r_out[...] = r
    kh_out[...] = kh
    v_out[...] = v
    kkn_out[...] = kk
    b_out[...] = kk * a
    lwf_out[...] = lwf
    lwb_out[...] = lwb
    g_out[...] = g
    bonus_out[...] = bonus


def _rwkv_prep(z, consts, bsz, t):
    n, zc = z.shape
    tb = min(PREP_TB, t)
    nblk = t // tb
    hb = tb // 8
    nrow8 = n // 8
    rmap = _seq_row_map(nblk, False)
    pmap = lambda b, i: (jnp.maximum((b * nblk + i) * hb - 1, 0), 0)
    nmap = lambda b, i: (jnp.minimum((b * nblk + i + 1) * hb, nrow8 - 1), 0)
    in_specs = [pl.BlockSpec((tb, zc), rmap), pl.BlockSpec((8, zc), pmap), pl.BlockSpec((8, zc), nmap)]
    in_specs += [pl.BlockSpec(c.shape, _const_map) for c in consts]
    return pl.pallas_call(
        _rwkv_prep_kernel,
        grid=(bsz, nblk),
        in_specs=in_specs,
        out_specs=[pl.BlockSpec((tb, MIX_W), rmap)] * 9,
        out_shape=[jax.ShapeDtypeStruct((n, MIX_W), F32)] * 9,
        compiler_params=_params(("parallel", "parallel")),
    )(z, z, z, *consts)


def _rwkv_scan_kernel(r_ref, lw_ref, kh_ref, v_ref, kk_ref, b_ref, bd_ref, mstrict_ref, mincl_ref,
                      o_ref, s_scr, *, reverse):
    @pl.when(pl.program_id(1) == 0)
    def _():
        s_scr[...] = jnp.zeros_like(s_scr)

    L = RWKV_CHUNK
    tb = r_ref.shape[0]
    nchunks = tb // L
    tri = _tri_incl(L)
    lane = lax.broadcasted_iota(jnp.int32, (1, LANE), 1)
    m0 = (lane < RWKV_HEAD).astype(F32)
    m1 = 1.0 - m0
    mstrict = mstrict_ref[...]
    mincl = mincl_ref[...]
    order = range(nchunks - 1, -1, -1) if reverse else range(nchunks)
    for c in order:
        sl = slice(c * L, (c + 1) * L)
        r, lw, kh, v, kk, b = (ref[sl, :] for ref in (r_ref, lw_ref, kh_ref, v_ref, kk_ref, b_ref))
        cum = _dot_exact_lhs(tri, lw)
        tot = cum[L - 1:L]
        inc = (tot - cum + lw) if reverse else cum
        exc = inc - lw
        mid = inc[L // 2:L // 2 + 1]
        e_pos = jnp.exp(inc - mid)
        e_neg = jnp.exp(mid - inc)
        rt = r * e_pos
        at = -kk * jnp.exp(exc - mid)
        bt = b * e_neg
        kt = kh * e_neg
        s = s_scr[...]
        am = _dot_nt(-kk * jnp.exp(exc), s)
        rm = _dot_nt(r * jnp.exp(inc), s)
        o_parts = []
        u_parts = []
        for p in range(MIX_W // LANE):
            ls = slice(p * LANE, (p + 1) * LANE)
            a_p, r_p, b_p, k_p, v_p = at[:, ls], rt[:, ls], bt[:, ls], kt[:, ls], v[:, ls]
            lhs = jnp.concatenate([a_p * m0, a_p * m1, r_p * m0, r_p * m1], axis=0)
            rhs = jnp.concatenate([b_p, b_p, k_p, k_p], axis=0)
            gram = _dot_nt(lhs, rhs)
            n_ab = gram[0:2 * L, 0:2 * L] * mstrict
            a_ak = gram[0:2 * L, 2 * L:4 * L] * mstrict
            a_rb = gram[2 * L:4 * L, 0:2 * L] * mincl
            a_rk = gram[2 * L:4 * L, 2 * L:4 * L] * mincl
            v_bd = jnp.concatenate([v_p * m0, v_p * m1], axis=0)
            am_p = am[:, ls]
            u = jnp.concatenate([am_p * m0, am_p * m1], axis=0) + _dot(a_ak, v_bd)
            pw = n_ab
            for it in range(6):
                u = u + _dot(pw, u)
                if it < 5:
                    pw = _dot(pw, pw)
            o_bd = _dot(a_rb, u) + _dot(a_rk, v_bd)
            o_parts.append(rm[:, ls] + o_bd[0:L] + o_bd[L:2 * L])
            u_parts.append(u[0:L] + u[L:2 * L])
        o_ref[sl, :] = jnp.concatenate(o_parts, axis=1)
        u_all = jnp.concatenate(u_parts, axis=1)
        e_end = jnp.exp(tot - inc)
        upd = _dot_tn(jnp.concatenate([u_all, v], axis=0),
                      jnp.concatenate([b * e_end, kh * e_end], axis=0))
        s_scr[...] = s * jnp.exp(tot) + upd * bd_ref[...]


def _rwkv_scan(r, lw, kh, v, kk, b, consts, bsz, t, reverse):
    n = r.shape[0]
    tb = min(RWKV_TB, t)
    nblk = t // tb
    rmap = _seq_row_map(nblk, reverse)
    row_spec = pl.BlockSpec((tb, MIX_W), rmap)
    return pl.pallas_call(
        functools.partial(_rwkv_scan_kernel, reverse=reverse),
        grid=(bsz, nblk),
        in_specs=[row_spec] * 6 + [pl.BlockSpec(c.shape, _const_map) for c in consts],
        out_specs=row_spec,
        out_shape=jax.ShapeDtypeStruct((n, MIX_W), F32),
        scratch_shapes=[pltpu.VMEM((MIX_W, MIX_W), F32)],
        compiler_params=_params(("parallel", "arbitrary")),
    )(r, lw, kh, v, kk, b, *consts)


def _merge_kernel(x_ref, ogla_ref, ohg_ref, orf_ref, orb_ref, bonus_ref, g_ref, zg_ref,
                  wg_ref, wh_ref, wr_ref, wo_ref, ng_ref, nb_ref, hh_ref, o_ref):
    o = orf_ref[...] + orb_ref[...]
    inv = 1.0 / RWKV_HEAD
    mean = _dot_exact_rhs(o, hh_ref[...]) * inv
    d = o - mean
    var = _dot_exact_rhs(d * d, hh_ref[...]) * inv
    on = d * lax.rsqrt(var + RWKV_GN_EPS) * ng_ref[...] + nb_ref[...]
    orw = (on + bonus_ref[...]) * g_ref[...]
    zg = zg_ref[...]
    gate = lambda j: jax.nn.sigmoid(zg[:, j * D_MODEL:(j + 1) * D_MODEL])
    merged = (gate(0) * _dot(ogla_ref[...], wg_ref[...])
              + gate(1) * _dot(ohg_ref[...], wh_ref[...])
              + gate(2) * _dot(orw, wr_ref[...]))
    o_ref[...] = x_ref[...] + _dot(merged, wo_ref[...])


def _merge(x, rows, consts):
    n, d = x.shape
    tm = min(MERGE_TILE, n)
    rspec = lambda a: pl.BlockSpec((tm, a.shape[1]), lambda i: (i, 0))
    return pl.pallas_call(
        _merge_kernel,
        grid=(n // tm,),
        in_specs=[rspec(x)] + [rspec(a) for a in rows]
        + [pl.BlockSpec(c.shape, lambda i: (0, 0)) for c in consts],
        out_specs=pl.BlockSpec((tm, d), lambda i: (i, 0)),
        out_shape=jax.ShapeDtypeStruct((n, d), F32),
        compiler_params=_params(("parallel",)),
    )(x, *rows, *consts)


def _xattn_kernel(x_ref, kv_ref, g_ref, wq_ref, wo_ref, o_ref):
    x = x_ref[...]
    h = _rmsnorm(x, g_ref[...])
    q = _dot(h, wq_ref[...])
    kv = kv_ref[...]
    outs = []
    for hd in range(X_HEADS):
        ls = slice(hd * X_HEAD, (hd + 1) * X_HEAD)
        s = _dot_nt(q[:, ls], kv[:, ls]) * (X_HEAD ** -0.5)
        s = s - jnp.max(s, axis=-1, keepdims=True)
        e = jnp.exp(s)
        pr = e / jnp.sum(e, axis=-1, keepdims=True)
        outs.append(_dot(pr, kv[:, D_MODEL + hd * X_HEAD:D_MODEL + (hd + 1) * X_HEAD]))
    o_ref[...] = x + _dot(jnp.concatenate(outs, axis=1), wo_ref[...])


def _xattn(x, kv, g, wq, wo, bsz, t, n_mem):
    n, d = x.shape
    tq = min(XATTN_TQ, t)
    nblk = t // tq
    return pl.pallas_call(
        _xattn_kernel,
        grid=(bsz, nblk),
        in_specs=[pl.BlockSpec((tq, d), lambda b, i: (b * nblk + i, 0)),
                  pl.BlockSpec((n_mem, 2 * d), lambda b, i: (b, 0)),
                  pl.BlockSpec((1, d), _const_map),
                  pl.BlockSpec((d, d), _const_map),
                  pl.BlockSpec((d, d), _const_map)],
        out_specs=pl.BlockSpec((tq, d), lambda b, i: (b * nblk + i, 0)),
        out_shape=jax.ShapeDtypeStruct((n, d), F32),
        compiler_params=_params(("parallel", "parallel")),
    )(x, kv, g.reshape(1, d), wq, wo)


def _top16_rows(x, payload):
    nrow = x.shape[0]
    rows = lax.broadcasted_iota(jnp.int32, x.shape, 0)
    vals, poss, pays = [], [], []
    for _ in range(PEER_TOPK):
        m = jnp.max(x, axis=0, keepdims=True)
        pos = jnp.min(jnp.where(x == m, rows, nrow), axis=0, keepdims=True)
        hit = rows == pos
        vals.append(m)
        poss.append(pos)
        pays.append(jnp.max(jnp.where(hit, payload, -1), axis=0, keepdims=True))
        x = jnp.where(hit, -jnp.inf, x)
    return jnp.concatenate(vals, 0), jnp.concatenate(poss, 0), jnp.concatenate(pays, 0)


def _peer_topk_kernel(q_ref, sk1_ref, sk2_ref, idx_ref, gate_ref):
    q = q_ref[...]
    tb = q.shape[0]
    keyrow = lax.broadcasted_iota(jnp.int32, (PEER_NKEYS, tb), 0)
    idx_parts, gate_parts = [], []
    for hd in range(PEER_HEADS):
        qh = q[:, hd * PEER_DK:(hd + 1) * PEER_DK]
        s1 = _dot_nt_f32(sk1_ref[...], qh)
        s2 = _dot_nt_f32(sk2_ref[...], qh)
        v1, _, i1 = _top16_rows(s1, keyrow)
        v2, _, i2 = _top16_rows(s2, keyrow)
        cand = jnp.concatenate([v1[a:a + 1] + v2 for a in range(PEER_TOPK)], axis=0)
        cidx = jnp.concatenate([i1[a:a + 1] * PEER_NKEYS + i2 for a in range(PEER_TOPK)], axis=0)
        top, _, eid = _top16_rows(cand, cidx)
        e = jnp.exp(top - jnp.max(top, axis=0, keepdims=True))
        gate_parts.append(e / jnp.sum(e, axis=0, keepdims=True))
        idx_parts.append(eid)
    idx_ref[...] = jnp.concatenate(idx_parts, axis=0).T
    gate_ref[...] = jnp.concatenate(gate_parts, axis=0).T


def _peer_topk(q, sk1p, sk2p):
    n, d = q.shape
    tb = min(TOPK_TB, n)
    nsel = PEER_HEADS * PEER_TOPK
    return pl.pallas_call(
        _peer_topk_kernel,
        grid=(n // tb,),
        in_specs=[pl.BlockSpec((tb, d), lambda i: (i, 0)),
                  pl.BlockSpec(sk1p.shape, lambda i: (0, 0)),
                  pl.BlockSpec(sk2p.shape, lambda i: (0, 0))],
        out_specs=[pl.BlockSpec((tb, nsel), lambda i: (i, 0))] * 2,
        out_shape=[jax.ShapeDtypeStruct((n, nsel), jnp.int32), jax.ShapeDtypeStruct((n, nsel), F32)],
        compiler_params=_params(("parallel",)),
    )(q, sk1p, sk2p)


def _peer_expert_kernel(idx_ref, idxn_ref, gate_ref, h_ref, x_ref, u_hbm, v_hbm, o_ref,
                        ubuf, vbuf, sem):
    i = pl.program_id(0)
    nstep = pl.num_programs(0)
    nsel = PEER_HEADS * PEER_TOPK
    nrow = PEER_TE * nsel

    def row_copy(table, buf, which, e, r, slot):
        return pltpu.make_async_copy(table.at[pl.ds(e, 1)], buf.at[slot, pl.ds(r, 1)],
                                     sem.at[which, slot])

    def issue(ids_ref, slot):
        def body(r, carry):
            e = ids_ref[r // nsel, r % nsel]
            row_copy(u_hbm, ubuf, 0, e, r, slot).start()
            row_copy(v_hbm, vbuf, 1, e, r, slot).start()
            return carry
        lax.fori_loop(0, nrow, body, 0)

    @pl.when(i == 0)
    def _():
        issue(idx_ref, 0)

    @pl.when(i + 1 < nstep)
    def _():
        issue(idxn_ref, (i + 1) % 2)

    slot = i % 2
    pltpu.make_async_copy(u_hbm.at[pl.ds(0, nrow)], ubuf.at[slot], sem.at[0, slot]).wait()
    pltpu.make_async_copy(v_hbm.at[pl.ds(0, nrow)], vbuf.at[slot], sem.at[1, slot]).wait()

    h = h_ref[...]
    act = _dot_nt(h, ubuf[slot])
    own = (lax.broadcasted_iota(jnp.int32, act.shape, 1) // nsel
           == lax.broadcasted_iota(jnp.int32, act.shape, 0))
    gate = jnp.tile(gate_ref[...], (1, PEER_TE))
    gelu = 0.5 * act * (1.0 + lax.erf(act * (2.0 ** -0.5)))
    coef = jnp.where(own, gate * gelu, 0.0)
    o_ref[...] = x_ref[...] + _dot(coef, vbuf[slot])


def _peer_experts(idx, gate, h, x, u, v):
    n, d = x.shape
    nsel = idx.shape[1]
    nstep = n // PEER_TE
    rspec = lambda c: pl.BlockSpec((PEER_TE, c), lambda i: (i, 0))
    return pl.pallas_call(
        _peer_expert_kernel,
        grid=(nstep,),
        in_specs=[pl.BlockSpec((PEER_TE, nsel), lambda i: (i, 0), memory_space=pltpu.SMEM),
                  pl.BlockSpec((PEER_TE, nsel), lambda i: (jnp.minimum(i + 1, nstep - 1), 0),
                               memory_space=pltpu.SMEM),
                  rspec(nsel), rspec(d), rspec(d),
                  pl.BlockSpec(memory_space=pl.ANY), pl.BlockSpec(memory_space=pl.ANY)],
        out_specs=rspec(d),
        out_shape=jax.ShapeDtypeStruct((n, d), F32),
        scratch_shapes=[pltpu.VMEM((2, PEER_TE * nsel, d), F32), pltpu.VMEM((2, PEER_TE * nsel, d), F32),
                        pltpu.SemaphoreType.DMA((2, 2))],
        compiler_params=_params(("arbitrary",)),
    )(idx, idx, gate, h, x, u, v)


def _pad_cols(a, width):
    return jnp.pad(a, ((0, 0), (0, width - a.shape[1])))


def _pad_rows(a, height):
    return jnp.pad(a, ((0, height - a.shape[0]), (0, 0)))


def _pack_rwkv_cols(a):
    o = 3 * MIX_W
    wf = a[:, o:o + RWKV_DECAY_RANK]
    wb = a[:, o + RWKV_DECAY_RANK:o + 2 * RWKV_DECAY_RANK]
    ad = a[:, o + 2 * RWKV_DECAY_RANK:o + 2 * RWKV_DECAY_RANK + RWKV_AAA_RANK]
    gd = a[:, o + 2 * RWKV_DECAY_RANK + RWKV_AAA_RANK:]
    return jnp.concatenate([a[:, :o], _pad_cols(wf, LANE), _pad_cols(wb, LANE), _pad_cols(ad, LANE),
                            _pad_cols(gd, LANE)], axis=1)


def _layer_weights(P, l, lb):
    w_in = P['w_in'][l]
    row = lambda a: a.reshape(1, -1).astype(F32)
    W = {}
    W['norm_mix_g'] = P['norm_mix_g'][l]
    W['w_gla'] = _pad_cols(w_in[:, :GLA_IN], ZGLA_W).astype(BF16)
    W['w_hgrn'] = w_in[:, GLA_IN:GLA_IN + HGRN_IN].astype(BF16)
    W['w_rwkv'] = _pack_rwkv_cols(w_in[:, GLA_IN + HGRN_IN:GATE_OFF]).astype(BF16)
    W['w_gate'] = w_in[:, GATE_OFF:].astype(BF16)
    W['gla_up_f'] = _pad_rows(P['gla_gate_up_f'][l], LANE).astype(BF16)
    W['gla_up_b'] = _pad_rows(jnp.concatenate(
        [jnp.zeros_like(P['gla_gate_up_b'][l]), P['gla_gate_up_b'][l]], axis=0), LANE).astype(BF16)
    W['gla_bias_f'] = row(P['gla_gate_bias_f'][l])
    W['gla_bias_b'] = row(P['gla_gate_bias_b'][l])
    W['gla_norm_g'] = row(P['gla_norm_g'][l])
    W['hgrn_lb'] = row(lb)
    W['hgrn_norm_g'] = row(P['hgrn_norm_g'][l])
    W['rwkv_mu_f'] = _pack_rwkv_cols(row(P['rwkv_mu_f'][l]))
    W['rwkv_mu_b'] = _pack_rwkv_cols(row(P['rwkv_mu_b'][l]))
    W['rwkv_w0_f'] = row(P['rwkv_w0_f'][l])
    W['rwkv_w2_f'] = _pad_rows(P['rwkv_w2_f'][l], LANE).astype(BF16)
    W['rwkv_w0_b'] = row(P['rwkv_w0_b'][l])
    W['rwkv_w2_b'] = _pad_rows(P['rwkv_w2_b'][l], LANE).astype(BF16)
    W['rwkv_a0'] = row(P['rwkv_a0'][l])
    W['rwkv_a2'] = _pad_rows(P['rwkv_a2'][l], LANE).astype(BF16)
    W['rwkv_g2'] = P['rwkv_g2'][l].astype(BF16)
    for name in ('rwkv_k_k', 'rwkv_k_a', 'rwkv_r_k', 'rwkv_norm_g', 'rwkv_norm_b'):
        W[name] = row(P[name][l])
    for name in ('w_branch_gla', 'w_branch_hgrn', 'w_branch_rwkv', 'w_out', 'xattn_wq', 'xattn_wo'):
        W[name] = P[name][l].astype(BF16)
    W['peer_wq'] = P['peer_wq'][l]
    W['xattn_wkv'] = jnp.concatenate([P['xattn_wk'][l], P['xattn_wv'][l]], axis=1).astype(BF16)
    for name in ('norm_x_g', 'norm_mem_g', 'norm_ffn_g'):
        W[name] = P[name][l]
    half = PEER_DK // 2
    W['peer_sk1'] = jnp.pad(P['peer_subkeys_1'][l], ((0, 0), (0, half)))
    W['peer_sk2'] = jnp.pad(P['peer_subkeys_2'][l], ((0, 0), (half, 0)))
    W['peer_u'] = P['peer_u'][l]
    W['peer_v'] = P['peer_v'][l]
    return W


def _shared_consts():
    L = RWKV_CHUNK
    idx = jnp.arange(2 * L)
    same = (idx[:, None] // L) == (idx[None, :] // L)
    tpos = idx % L
    C = {
        'gla_ebc': _head_match(GLA_W, GLA_DK, MIX_W, GLA_DV, BF16),
        'gla_bdt': _head_match(MIX_W, GLA_DV, GLA_W, GLA_DK, F32),
        'hgrn_ebc': _head_match(HGRN_W, HGRN_DK, MIX_W, HGRN_DV, BF16),
        'hgrn_bdt': _head_match(MIX_W, HGRN_DV, HGRN_W, HGRN_DK, F32),
        'hh128': _head_match(MIX_W, 128, MIX_W, 128, BF16),
        'hh64': _head_match(MIX_W, RWKV_HEAD, MIX_W, RWKV_HEAD, BF16),
        'rwkv_bd': _head_match(MIX_W, RWKV_HEAD, MIX_W, RWKV_HEAD, F32),
        'strict_f': (same & (tpos[None, :] < tpos[:, None])).astype(F32),
        'incl_f': (same & (tpos[None, :] <= tpos[:, None])).astype(F32),
        'strict_b': (same & (tpos[None, :] > tpos[:, None])).astype(F32),
        'incl_b': (same & (tpos[None, :] >= tpos[:, None])).astype(F32),
    }
    return C


def _encoder_layer(x, mem, W, C, bsz, t, n_mem):
    g_mix = W['norm_mix_g']
    z_gla = _norm_matmul(x, g_mix, W['w_gla'])
    z_hgrn = _norm_matmul(x, g_mix, W['w_hgrn'])
    z_rwkv = _norm_matmul(x, g_mix, W['w_rwkv'])
    z_gate = _norm_matmul(x, g_mix, W['w_gate'])

    gla_c = (C['gla_ebc'], C['gla_bdt'])
    o_f = _lin_attn_pass(_gla_kernel, z_gla, (W['gla_up_f'], W['gla_bias_f']) + gla_c, None, None,
                         bsz, t, GLA_W, False)
    o_gla = _lin_attn_pass(_gla_kernel, z_gla, (W['gla_up_b'], W['gla_bias_b']) + gla_c, o_f,
                           (C['hh128'], W['gla_norm_g']), bsz, t, GLA_W, True)

    hg_c = (W['hgrn_lb'], C['hgrn_ebc'], C['hgrn_bdt'])
    o_f = _lin_attn_pass(_hgrn_kernel, z_hgrn, hg_c, None, None, bsz, t, HGRN_W, False)
    o_hgrn = _lin_attn_pass(_hgrn_kernel, z_hgrn, hg_c, o_f, (C['hh128'], W['hgrn_norm_g']),
                            bsz, t, HGRN_W, True)

    prep_c = (W['rwkv_mu_f'], W['rwkv_mu_b'], W['rwkv_w0_f'], W['rwkv_w2_f'], W['rwkv_w0_b'],
              W['rwkv_w2_b'], W['rwkv_a0'], W['rwkv_a2'], W['rwkv_g2'], W['rwkv_k_k'], W['rwkv_k_a'],
              W['rwkv_r_k'], C['hh64'])
    r, kh, v, kk, b, lw_f, lw_b, g, bonus = _rwkv_prep(z_rwkv, prep_c, bsz, t)
    o_rf = _rwkv_scan(r, lw_f, kh, v, kk, b, (C['rwkv_bd'], C['strict_f'], C['incl_f']), bsz, t, False)
    o_rb = _rwkv_scan(r, lw_b, kh, v, kk, b, (C['rwkv_bd'], C['strict_b'], C['incl_b']), bsz, t, True)

    x = _merge(x, (o_gla, o_hgrn, o_rf, o_rb, bonus, g, z_gate),
               (W['w_branch_gla'], W['w_branch_hgrn'], W['w_branch_rwkv'], W['w_out'],
                W['rwkv_norm_g'], W['rwkv_norm_b'], C['hh64']))

    kv = _norm_matmul(mem, W['norm_mem_g'], W['xattn_wkv'])
    x = _xattn(x, kv, W['norm_x_g'], W['xattn_wq'], W['xattn_wo'], bsz, t, n_mem)

    q, h = _norm_matmul(x, W['norm_ffn_g'], W['peer_wq'], emit_h=True)
    idx, gate = _peer_topk(q, W['peer_sk1'], W['peer_sk2'])
    return _peer_experts(idx, gate, h, x, W['peer_u'], W['peer_v'])


def _run_trunk(x, mem, P, weights, C):
    bsz, t, d = x.shape
    n_mem = mem.shape[1]
    x = x.reshape(bsz * t, d)
    mem = mem.reshape(bsz * n_mem, d)
    for W in weights:
        x = _encoder_layer(x, mem, W, C, bsz, t, n_mem)
    return _final_norm(x, P['final_norm_g']).reshape(bsz, t, d)


def _hgrn_lower_bounds(logits):
    sm = jax.nn.softmax(logits.astype(F32), axis=0)
    return jnp.cumsum(sm, axis=0) - sm[0]


def kernel(x_prompt, x_sample, mem_prompt, mem_sample, norm_mix_g, w_in, gla_gate_up_f, gla_gate_up_b, gla_gate_bias_f, gla_gate_bias_b, gla_norm_g, hgrn_lb_logits, hgrn_norm_g, rwkv_mu_f, rwkv_mu_b, rwkv_w0_f, rwkv_w2_f, rwkv_w0_b, rwkv_w2_b, rwkv_a0, rwkv_a2, rwkv_g2, rwkv_k_k, rwkv_k_a, rwkv_r_k, rwkv_norm_g, rwkv_norm_b, w_branch_gla, w_branch_hgrn, w_branch_rwkv, w_out, norm_x_g, norm_mem_g, xattn_wq, xattn_wk, xattn_wv, xattn_wo, norm_ffn_g, peer_wq, peer_subkeys_1, peer_subkeys_2, peer_u, peer_v, final_norm_g):
    P = dict(norm_mix_g=norm_mix_g, w_in=w_in, gla_gate_up_f=gla_gate_up_f, gla_gate_up_b=gla_gate_up_b,
             gla_gate_bias_f=gla_gate_bias_f, gla_gate_bias_b=gla_gate_bias_b, gla_norm_g=gla_norm_g,
             hgrn_lb_logits=hgrn_lb_logits, hgrn_norm_g=hgrn_norm_g, rwkv_mu_f=rwkv_mu_f, rwkv_mu_b=rwkv_mu_b,
             rwkv_w0_f=rwkv_w0_f, rwkv_w2_f=rwkv_w2_f, rwkv_w0_b=rwkv_w0_b, rwkv_w2_b=rwkv_w2_b,
             rwkv_a0=rwkv_a0, rwkv_a2=rwkv_a2, rwkv_g2=rwkv_g2, rwkv_k_k=rwkv_k_k, rwkv_k_a=rwkv_k_a,
             rwkv_r_k=rwkv_r_k, rwkv_norm_g=rwkv_norm_g, rwkv_norm_b=rwkv_norm_b, w_branch_gla=w_branch_gla,
             w_branch_hgrn=w_branch_hgrn, w_branch_rwkv=w_branch_rwkv, w_out=w_out, norm_x_g=norm_x_g,
             norm_mem_g=norm_mem_g, xattn_wq=xattn_wq, xattn_wk=xattn_wk, xattn_wv=xattn_wv, xattn_wo=xattn_wo,
             norm_ffn_g=norm_ffn_g, peer_wq=peer_wq, peer_subkeys_1=peer_subkeys_1,
             peer_subkeys_2=peer_subkeys_2, peer_u=peer_u, peer_v=peer_v, final_norm_g=final_norm_g)
    depth = w_in.shape[0]
    lbs = _hgrn_lower_bounds(hgrn_lb_logits)
    weights = [_layer_weights(P, l, lbs[l]) for l in range(depth)]
    C = _shared_consts()
    return (_run_trunk(x_prompt, mem_prompt, P, weights, C),
            _run_trunk(x_sample, mem_sample, P, weights, C))
```

```python
import functools

import jax
import jax.numpy as jnp
from jax import lax
from jax.experimental import pallas as pl
from jax.experimental.pallas import tpu as pltpu

F32 = jnp.float32
BF16 = jnp.bfloat16

D_MODEL = 1024
EPS = 1e-6
LOG_FLOOR = 1e-30
MIX_W = 512
GLA_HEADS, GLA_DK, GLA_DV = 4, 64, 128
GLA_GATE_RANK = 16
GLA_GATE_NORM = 16.0
HGRN_HEADS, HGRN_DK, HGRN_DV = 4, 128, 128
RWKV_HEADS, RWKV_HEAD = 8, 64
RWKV_DECAY_RANK, RWKV_AAA_RANK, RWKV_GATE_RANK = 64, 64, 128
RWKV_DECAY_SCALE = 0.606531
RWKV_GN_EPS = 64e-5
X_HEADS = 4
X_HEAD = D_MODEL // X_HEADS
PEER_HEADS, PEER_DK, PEER_NKEYS, PEER_TOPK = 8, 128, 128, 16

GLA_W = GLA_HEADS * GLA_DK
GLA_IN = 2 * GLA_W + 2 * MIX_W + 2 * GLA_GATE_RANK
HGRN_W = HGRN_HEADS * HGRN_DK
HGRN_IN = 5 * MIX_W
RWKV_IN = 3 * MIX_W + 2 * RWKV_DECAY_RANK + RWKV_AAA_RANK + RWKV_GATE_RANK
GATE_OFF = GLA_IN + HGRN_IN + RWKV_IN

LANE = 128
ZGLA_W = 2 * GLA_W + 2 * MIX_W + LANE
ZRWKV_W = 3 * MIX_W + 4 * LANE

ROW_TILE = 512
MERGE_TILE = 256
GLA_CHUNK = 32
GLA_TB = 256
RWKV_CHUNK = 64
RWKV_TB = 128
PREP_TB = 256
XATTN_TQ = 256
TOPK_TB = 128
PEER_TB = 256
PEER_ET = 1024
VMEM_LIMIT = 48 * 1024 * 1024


def _params(sem):
    return pltpu.CompilerParams(dimension_semantics=sem, vmem_limit_bytes=VMEM_LIMIT)


def _dot(a, b):
    return jnp.dot(a.astype(BF16), b.astype(BF16), preferred_element_type=F32)


def _dot_nt(a, b):
    return lax.dot_general(a.astype(BF16), b.astype(BF16), (((1,), (1,)), ((), ())),
                           preferred_element_type=F32)


def _dot_nt_f32(a, b):
    return lax.dot_general(a, b, (((1,), (1,)), ((), ())), precision=lax.Precision.HIGHEST,
                           preferred_element_type=F32)


def _dot_tn(a, b):
    return lax.dot_general(a.astype(BF16), b.astype(BF16), (((0,), (0,)), ((), ())),
                           preferred_element_type=F32)


def _split3(x):
    hi = x.astype(BF16)
    r1 = x - hi.astype(F32)
    mid = r1.astype(BF16)
    lo = (r1 - mid.astype(F32)).astype(BF16)
    return hi, mid, lo


def _dot_exact_rhs(x, m):
    hi, mid, lo = _split3(x)
    f = lambda p: jnp.dot(p, m, preferred_element_type=F32)
    return f(hi) + f(mid) + f(lo)


def _dot_exact_lhs(m, x):
    hi, mid, lo = _split3(x)
    f = lambda p: jnp.dot(m, p, preferred_element_type=F32)
    return f(hi) + f(mid) + f(lo)


def _tri_incl(n):
    r = lax.broadcasted_iota(jnp.int32, (n, n), 0)
    c = lax.broadcasted_iota(jnp.int32, (n, n), 1)
    return (r >= c).astype(BF16)


def _rmsnorm(x, g):
    return x * lax.rsqrt(jnp.mean(x * x, axis=-1, keepdims=True) + EPS) * g


def _log_sigmoid(x):
    return jnp.minimum(x, 0.0) - jnp.log(1.0 + jnp.exp(-jnp.abs(x)))


def _silu(x):
    return x * jax.nn.sigmoid(x)


def _norm_matmul_kernel(x_ref, g_ref, w_ref, o_ref, *h_ref):
    h = _rmsnorm(x_ref[...], g_ref[...])
    if w_ref.dtype == F32:
        o_ref[...] = jnp.dot(h, w_ref[...], precision=lax.Precision.HIGHEST, preferred_element_type=F32)
    else:
        o_ref[...] = jnp.dot(h.astype(BF16), w_ref[...], preferred_element_type=F32)
    if h_ref:
        h_ref[0][...] = h.astype(h_ref[0].dtype)


def _norm_matmul(x, g, w, emit_h=False):
    n, d = x.shape
    c = w.shape[1]
    tm = min(ROW_TILE, n)
    out_shape = [jax.ShapeDtypeStruct((n, c), F32)]
    out_specs = [pl.BlockSpec((tm, c), lambda i: (i, 0))]
    if emit_h:
        out_shape.append(jax.ShapeDtypeStruct((n, d), BF16))
        out_specs.append(pl.BlockSpec((tm, d), lambda i: (i, 0)))
    res = pl.pallas_call(
        _norm_matmul_kernel,
        grid=(n // tm,),
        in_specs=[pl.BlockSpec((tm, d), lambda i: (i, 0)),
                  pl.BlockSpec((1, d), lambda i: (0, 0)),
                  pl.BlockSpec((d, c), lambda i: (0, 0))],
        out_specs=out_specs,
        out_shape=out_shape,
        compiler_params=_params(("parallel",)),
    )(x, g.reshape(1, d), w)
    return res if emit_h else res[0]


def _final_norm_kernel(x_ref, g_ref, o_ref):
    o_ref[...] = _rmsnorm(x_ref[...], g_ref[...])


def _final_norm(x, g):
    n, d = x.shape
    tm = min(ROW_TILE, n)
    return pl.pallas_call(
        _final_norm_kernel,
        grid=(n // tm,),
        in_specs=[pl.BlockSpec((tm, d), lambda i: (i, 0)), pl.BlockSpec((1, d), lambda i: (0, 0))],
        out_specs=pl.BlockSpec((tm, d), lambda i: (i, 0)),
        out_shape=jax.ShapeDtypeStruct((n, d), F32),
        compiler_params=_params(("parallel",)),
    )(x, g.reshape(1, d))


def _gla_time_block(q, k, v, g, o_scr, st_scr, qs, ks, vs, cs, ebc_ref, bdt_ref, tri_ref, blk_ref,
                    reverse):
    tb = q.shape[0]
    nchunks = tb // GLA_CHUNK
    cum = _dot_exact_lhs(tri_ref[...], g)
    tot = _dot_exact_lhs(blk_ref[...], g)
    if reverse:
        cum = tot - cum + g
    qs[...] = q
    ks[...] = k
    vs[...] = v
    cs[...] = cum
    pos = lax.broadcasted_iota(jnp.int32, (tb, 1), 0) % GLA_CHUNK

    def body(d, o):
        shift = ((tb - d) % tb) if reverse else d
        kr = pltpu.roll(ks[...], shift, 0)
        cr = pltpu.roll(cs[...], shift, 0)
        vr = pltpu.roll(vs[...], shift, 0)
        live = (pos <= GLA_CHUNK - 1 - d) if reverse else (pos >= d)
        a = jnp.where(live, qs[...] * kr * jnp.exp(jnp.minimum(cs[...] - cr, 0.0)), 0.0)
        return o + jnp.dot(a.astype(BF16), ebc_ref[...], preferred_element_type=F32) * vr

    o_scr[...] = lax.fori_loop(0, GLA_CHUNK, body, jnp.zeros(v.shape, F32))
    qe = q * jnp.exp(cum)
    kd = k * jnp.exp(tot - cum)
    dec = jnp.exp(tot)
    order = range(nchunks - 1, -1, -1) if reverse else range(nchunks)
    for c in order:
        sl = slice(c * GLA_CHUNK, (c + 1) * GLA_CHUNK)
        st = st_scr[...]
        o_scr[sl, :] = o_scr[sl, :] + _dot_nt(qe[sl], st)
        st_scr[...] = (st * dec[c * GLA_CHUNK:c * GLA_CHUNK + 1]
                       + _dot_tn(v[sl], kd[sl]) * bdt_ref[...])


def _head_rms_gate(o, og, hh_ref, ng_ref, dv):
    ms = _dot_exact_rhs(o * o, hh_ref[...]) * (1.0 / dv)
    return o * lax.rsqrt(ms + EPS) * ng_ref[...] * _silu(og)


def _gla_kernel(*refs, reverse, final):
    if final:
        (z_ref, up_ref, bias_ref, ebc_ref, bdt_ref, tri_ref, blk_ref, oprev_ref, hh_ref, ng_ref,
         o_ref, st_scr, o_scr, qs, ks, vs, cs) = refs
    else:
        (z_ref, up_ref, bias_ref, ebc_ref, bdt_ref, tri_ref, blk_ref,
         o_ref, st_scr, o_scr, qs, ks, vs, cs) = refs

    @pl.when(pl.program_id(1) == 0)
    def _():
        st_scr[...] = jnp.zeros_like(st_scr)

    z = z_ref[...]
    q = z[:, 0:GLA_W] * (GLA_DK ** -0.5)
    k = z[:, GLA_W:2 * GLA_W]
    v = z[:, 2 * GLA_W:2 * GLA_W + MIX_W]
    gd = z[:, 2 * GLA_W + 2 * MIX_W:ZGLA_W]
    g = _log_sigmoid(_dot(gd, up_ref[...]) + bias_ref[...]) * (1.0 / GLA_GATE_NORM)
    _gla_time_block(q, k, v, g, o_scr, st_scr, qs, ks, vs, cs, ebc_ref, bdt_ref, tri_ref, blk_ref, reverse)
    if final:
        og = z[:, 2 * GLA_W + MIX_W:2 * GLA_W + 2 * MIX_W]
        o_ref[...] = _head_rms_gate(oprev_ref[...] + o_scr[...], og, hh_ref, ng_ref, GLA_DV)
    else:
        o_ref[...] = o_scr[...]


def _hgrn_kernel(*refs, reverse, final):
    if final:
        (z_ref, lb_ref, ebc_ref, bdt_ref, tri_ref, blk_ref, oprev_ref, hh_ref, ng_ref,
         o_ref, st_scr, o_scr, qs, ks, vs, cs) = refs
    else:
        z_ref, lb_ref, ebc_ref, bdt_ref, tri_ref, blk_ref, o_ref, st_scr, o_scr, qs, ks, vs, cs = refs

    @pl.when(pl.program_id(1) == 0)
    def _():
        st_scr[...] = jnp.zeros_like(st_scr)

    z = z_ref[...]
    lb = lb_ref[...]
    q = _silu(z[:, 0:HGRN_W])
    zf = z[:, (2 if reverse else 1) * HGRN_W:(3 if reverse else 2) * HGRN_W]
    v = z[:, 3 * HGRN_W:4 * HGRN_W]
    f = lb + (1.0 - lb) * jax.nn.sigmoid(zf)
    g = jnp.log(jnp.maximum(f, LOG_FLOOR))
    k = (1.0 - lb) * jax.nn.sigmoid(-zf)
    _gla_time_block(q, k, v, g, o_scr, st_scr, qs, ks, vs, cs, ebc_ref, bdt_ref, tri_ref, blk_ref, reverse)
    if final:
        og = z[:, 4 * HGRN_W:5 * HGRN_W]
        o_ref[...] = _head_rms_gate(oprev_ref[...] + o_scr[...], og, hh_ref, ng_ref, HGRN_DV)
    else:
        o_ref[...] = o_scr[...]


def _seq_row_map(nblk, reverse):
    if reverse:
        return lambda b, i: (b * nblk + nblk - 1 - i, 0)
    return lambda b, i: (b * nblk + i, 0)


def _const_map(b, i):
    return (0, 0)


def _lin_attn_pass(kernel, z, consts, o_prev, final_consts, bsz, t, w, reverse):
    n, zc = z.shape
    tb = min(GLA_TB, t)
    nblk = t // tb
    rmap = _seq_row_map(nblk, reverse)
    final = o_prev is not None
    r = jnp.arange(tb)
    same = (r[:, None] // GLA_CHUNK) == (r[None, :] // GLA_CHUNK)
    consts = tuple(consts) + ((same & (r[None, :] <= r[:, None])).astype(BF16), same.astype(BF16))
    args = [z] + list(consts)
    in_specs = [pl.BlockSpec((tb, zc), rmap)] + [pl.BlockSpec(c.shape, _const_map) for c in consts]
    if final:
        args += [o_prev] + list(final_consts)
        in_specs += [pl.BlockSpec((tb, MIX_W), rmap)]
        in_specs += [pl.BlockSpec(c.shape, _const_map) for c in final_consts]
    return pl.pallas_call(
        functools.partial(kernel, reverse=reverse, final=final),
        grid=(bsz, nblk),
        in_specs=in_specs,
        out_specs=pl.BlockSpec((tb, MIX_W), rmap),
        out_shape=jax.ShapeDtypeStruct((n, MIX_W), F32),
        scratch_shapes=[pltpu.VMEM((MIX_W, w), F32), pltpu.VMEM((tb, MIX_W), F32),
                        pltpu.VMEM((tb, w), F32), pltpu.VMEM((tb, w), F32),
                        pltpu.VMEM((tb, MIX_W), F32), pltpu.VMEM((tb, w), F32)],
        compiler_params=_params(("parallel", "arbitrary")),
    )(*args)


def _head_match(n_rows, row_blk, n_cols, col_blk, dtype):
    r = jnp.arange(n_rows) // row_blk
    c = jnp.arange(n_cols) // col_blk
    return (r[:, None] == c[None, :]).astype(dtype)


def _rwkv_prep_kernel(z_ref, zp_ref, zn_ref, muf_ref, mub_ref, w0f_ref, w2f_ref, w0b_ref, w2b_ref,
                      a0_ref, a2_ref, g2_ref, kk_ref, ka_ref, rk_ref, hh_ref,
                      r_out, kh_out, v_out, kkn_out, b_out, lwf_out, lwb_out, g_out, bonus_out):
    i = pl.program_id(1)
    last = pl.num_programs(1) - 1
    z = z_ref[...]
    tb = z.shape[0]
    rows = lax.broadcasted_iota(jnp.int32, (tb, 1), 0)
    hp = jnp.where(i == 0, 0.0, zp_ref[7:8, :])
    hn = jnp.where(i == last, 0.0, zn_ref[0:1, :])
    prev = jnp.where(rows == 0, hp, pltpu.roll(z, 1, 0))
    nxt = jnp.where(rows == tb - 1, hn, pltpu.roll(z, tb - 1, 0))
    p = z + muf_ref[...] * (prev - z) + mub_ref[...] * (nxt - z)
    r = p[:, 0:MIX_W]
    k = p[:, MIX_W:2 * MIX_W]
    v = p[:, 2 * MIX_W:3 * MIX_W]
    o = 3 * MIX_W
    wdf, wdb, ad, gd = (p[:, o + j * LANE:o + (j + 1) * LANE] for j in range(4))
    lwf = -RWKV_DECAY_SCALE * jax.nn.sigmoid(w0f_ref[...] + _dot(jnp.tanh(wdf), w2f_ref[...]))
    lwb = -RWKV_DECAY_SCALE * jax.nn.sigmoid(w0b_ref[...] + _dot(jnp.tanh(wdb), w2b_ref[...]))
    a = jax.nn.sigmoid(a0_ref[...] + _dot(ad, a2_ref[...]))
    g = _dot(jax.nn.sigmoid(gd), g2_ref[...])
    kk = k * kk_ref[...]
    ss = _dot_exact_rhs(kk * kk, hh_ref[...])
    kk = kk / jnp.maximum(jnp.sqrt(ss), 1e-12)
    kh = k * (1.0 + (a - 1.0) * ka_ref[...])
    bonus = _dot_exact_rhs(r * kh * rk_ref[...], hh_ref[...]) * v
    r_out[...] = r
    kh_out[...] = kh
    v_out[...] = v
    kkn_out[...] = kk
    b_out[...] = kk * a
    lwf_out[...] = lwf
    lwb_out[...] = lwb
    g_out[...] = g
    bonus_out[...] = bonus


def _rwkv_prep(z, consts, bsz, t):
    n, zc = z.shape
    tb = min(PREP_TB, t)
    nblk = t // tb
    hb = tb // 8
    nrow8 = n // 8
    rmap = _seq_row_map(nblk, False)
    pmap = lambda b, i: (jnp.maximum((b * nblk + i) * hb - 1, 0), 0)
    nmap = lambda b, i: (jnp.minimum((b * nblk + i + 1) * hb, nrow8 - 1), 0)
    in_specs = [pl.BlockSpec((tb, zc), rmap), pl.BlockSpec((8, zc), pmap), pl.BlockSpec((8, zc), nmap)]
    in_specs += [pl.BlockSpec(c.shape, _const_map) for c in consts]
    return pl.pallas_call(
        _rwkv_prep_kernel,
        grid=(bsz, nblk),
        in_specs=in_specs,
        out_specs=[pl.BlockSpec((tb, MIX_W), rmap)] * 9,
        out_shape=[jax.ShapeDtypeStruct((n, MIX_W), F32)] * 9,
        compiler_params=_params(("parallel", "parallel")),
    )(z, z, z, *consts)


def _rwkv_scan_kernel(r_ref, lw_ref, kh_ref, v_ref, kk_ref, b_ref, bd_ref, mstrict_ref, mincl_ref,
                      o_ref, s_scr, *, reverse):
    @pl.when(pl.program_id(1) == 0)
    def _():
        s_scr[...] = jnp.zeros_like(s_scr)

    L = RWKV_CHUNK
    tb = r_ref.shape[0]
    nchunks = tb // L
    tri = _tri_incl(L)
    lane = lax.broadcasted_iota(jnp.int32, (1, LANE), 1)
    m0 = (lane < RWKV_HEAD).astype(F32)
    m1 = 1.0 - m0
    mstrict = mstrict_ref[...]
    mincl = mincl_ref[...]
    order = range(nchunks - 1, -1, -1) if reverse else range(nchunks)
    for c in order:
        sl = slice(c * L, (c + 1) * L)
        r, lw, kh, v, kk, b = (ref[sl, :] for ref in (r_ref, lw_ref, kh_ref, v_ref, kk_ref, b_ref))
        cum = _dot_exact_lhs(tri, lw)
        tot = cum[L - 1:L]
        inc = (tot - cum + lw) if reverse else cum
        exc = inc - lw
        mid = inc[L // 2:L // 2 + 1]
        e_pos = jnp.exp(inc - mid)
        e_neg = jnp.exp(mid - inc)
        rt = r * e_pos
        at = -kk * jnp.exp(exc - mid)
        bt = b * e_neg
        kt = kh * e_neg
        s = s_scr[...]
        am = _dot_nt(-kk * jnp.exp(exc), s)
        rm = _dot_nt(r * jnp.exp(inc), s)
        o_parts = []
        u_parts = []
        for p in range(MIX_W // LANE):
            ls = slice(p * LANE, (p + 1) * LANE)
            a_p, r_p, b_p, k_p, v_p = at[:, ls], rt[:, ls], bt[:, ls], kt[:, ls], v[:, ls]
            lhs = jnp.concatenate([a_p * m0, a_p * m1, r_p * m0, r_p * m1], axis=0)
            rhs = jnp.concatenate([b_p, b_p, k_p, k_p], axis=0)
            gram = _dot_nt(lhs, rhs)
            n_ab = gram[0:2 * L, 0:2 * L] * mstrict
            a_ak = gram[0:2 * L, 2 * L:4 * L] * mstrict
            a_rb = gram[2 * L:4 * L, 0:2 * L] * mincl
            a_rk = gram[2 * L:4 * L, 2 * L:4 * L] * mincl
            v_bd = jnp.concatenate([v_p * m0, v_p * m1], axis=0)
            am_p = am[:, ls]
            u = jnp.concatenate([am_p * m0, am_p * m1], axis=0) + _dot(a_ak, v_bd)
            pw = n_ab
            for it in range(6):
                u = u + _dot(pw, u)
                if it < 5:
                    pw = _dot(pw, pw)
            o_bd = _dot(a_rb, u) + _dot(a_rk, v_bd)
            o_parts.append(rm[:, ls] + o_bd[0:L] + o_bd[L:2 * L])
            u_parts.append(u[0:L] + u[L:2 * L])
        o_ref[sl, :] = jnp.concatenate(o_parts, axis=1)
        u_all = jnp.concatenate(u_parts, axis=1)
        e_end = jnp.exp(tot - inc)
        upd = _dot_tn(jnp.concatenate([u_all, v], axis=0),
                      jnp.concatenate([b * e_end, kh * e_end], axis=0))
        s_scr[...] = s * jnp.exp(tot) + upd * bd_ref[...]


def _rwkv_scan(r, lw, kh, v, kk, b, consts, bsz, t, reverse):
    n = r.shape[0]
    tb = min(RWKV_TB, t)
    nblk = t // tb
    rmap = _seq_row_map(nblk, reverse)
    row_spec = pl.BlockSpec((tb, MIX_W), rmap)
    return pl.pallas_call(
        functools.partial(_rwkv_scan_kernel, reverse=reverse),
        grid=(bsz, nblk),
        in_specs=[row_spec] * 6 + [pl.BlockSpec(c.shape, _const_map) for c in consts],
        out_specs=row_spec,
        out_shape=jax.ShapeDtypeStruct((n, MIX_W), F32),
        scratch_shapes=[pltpu.VMEM((MIX_W, MIX_W), F32)],
        compiler_params=_params(("parallel", "arbitrary")),
    )(r, lw, kh, v, kk, b, *consts)


def _merge_kernel(x_ref, ogla_ref, ohg_ref, orf_ref, orb_ref, bonus_ref, g_ref, zg_ref,
                  wg_ref, wh_ref, wr_ref, wo_ref, ng_ref, nb_ref, hh_ref, o_ref):
    o = orf_ref[...] + orb_ref[...]
    inv = 1.0 / RWKV_HEAD
    mean = _dot_exact_rhs(o, hh_ref[...]) * inv
    d = o - mean
    var = _dot_exact_rhs(d * d, hh_ref[...]) * inv
    on = d * lax.rsqrt(var + RWKV_GN_EPS) * ng_ref[...] + nb_ref[...]
    orw = (on + bonus_ref[...]) * g_ref[...]
    zg = zg_ref[...]
    gate = lambda j: jax.nn.sigmoid(zg[:, j * D_MODEL:(j + 1) * D_MODEL])
    merged = (gate(0) * _dot(ogla_ref[...], wg_ref[...])
              + gate(1) * _dot(ohg_ref[...], wh_ref[...])
              + gate(2) * _dot(orw, wr_ref[...]))
    o_ref[...] = x_ref[...] + _dot(merged, wo_ref[...])


def _merge(x, rows, consts):
    n, d = x.shape
    tm = min(MERGE_TILE, n)
    rspec = lambda a: pl.BlockSpec((tm, a.shape[1]), lambda i: (i, 0))
    return pl.pallas_call(
        _merge_kernel,
        grid=(n // tm,),
        in_specs=[rspec(x)] + [rspec(a) for a in rows]
        + [pl.BlockSpec(c.shape, lambda i: (0, 0)) for c in consts],
        out_specs=pl.BlockSpec((tm, d), lambda i: (i, 0)),
        out_shape=jax.ShapeDtypeStruct((n, d), F32),
        compiler_params=_params(("parallel",)),
    )(x, *rows, *consts)


def _xattn_kernel(x_ref, kv_ref, g_ref, wq_ref, wo_ref, o_ref):
    x = x_ref[...]
    h = _rmsnorm(x, g_ref[...])
    q = _dot(h, wq_ref[...])
    kv = kv_ref[...]
    outs = []
    for hd in range(X_HEADS):
        ls = slice(hd * X_HEAD, (hd + 1) * X_HEAD)
        s = _dot_nt(q[:, ls], kv[:, ls]) * (X_HEAD ** -0.5)
        s = s - jnp.max(s, axis=-1, keepdims=True)
        e = jnp.exp(s)
        pr = e / jnp.sum(e, axis=-1, keepdims=True)
        outs.append(_dot(pr, kv[:, D_MODEL + hd * X_HEAD:D_MODEL + (hd + 1) * X_HEAD]))
    o_ref[...] = x + _dot(jnp.concatenate(outs, axis=1), wo_ref[...])


def _xattn(x, kv, g, wq, wo, bsz, t, n_mem):
    n, d = x.shape
    tq = min(XATTN_TQ, t)
    nblk = t // tq
    return pl.pallas_call(
        _xattn_kernel,
        grid=(bsz, nblk),
        in_specs=[pl.BlockSpec((tq, d), lambda b, i: (b * nblk + i, 0)),
                  pl.BlockSpec((n_mem, 2 * d), lambda b, i: (b, 0)),
                  pl.BlockSpec((1, d), _const_map),
                  pl.BlockSpec((d, d), _const_map),
                  pl.BlockSpec((d, d), _const_map)],
        out_specs=pl.BlockSpec((tq, d), lambda b, i: (b * nblk + i, 0)),
        out_shape=jax.ShapeDtypeStruct((n, d), F32),
        compiler_params=_params(("parallel", "parallel")),
    )(x, kv, g.reshape(1, d), wq, wo)


def _top16_rows(x):
    nrow = x.shape[0]
    rows = lax.broadcasted_iota(jnp.int32, x.shape, 0)
    vals = []
    for _ in range(PEER_TOPK):
        m = jnp.max(x, axis=0, keepdims=True)
        pos = jnp.min(jnp.where(x == m, rows, nrow), axis=0, keepdims=True)
        vals.append(m)
        x = jnp.where(rows == pos, -jnp.inf, x)
    return jnp.concatenate(vals, 0)


def _peer_topk_kernel(q_ref, sk1_ref, sk2_ref, s1_ref, s2_ref, st_ref):
    q = q_ref[...]
    thr, mx1, mx2, rz = [], [], [], []
    for hd in range(PEER_HEADS):
        hs = slice(hd * PEER_NKEYS, (hd + 1) * PEER_NKEYS)
        qh = q[:, hd * PEER_DK:(hd + 1) * PEER_DK]
        s1 = _dot_nt_f32(sk1_ref[...], qh)
        s2 = _dot_nt_f32(sk2_ref[...], qh)
        s1_ref[hs, :] = s1
        s2_ref[hs, :] = s2
        v1 = _top16_rows(s1)
        v2 = _top16_rows(s2)
        cand = jnp.concatenate([v1[a:a + 1] + v2 for a in range(PEER_TOPK)], axis=0)
        top = _top16_rows(cand)
        thr.append(top[PEER_TOPK - 1:PEER_TOPK])
        mx1.append(v1[0:1])
        mx2.append(v2[0:1])
        rz.append(1.0 / jnp.sum(jnp.exp(top - top[0:1]), axis=0, keepdims=True))
    st_ref[...] = jnp.concatenate(thr + mx1 + mx2 + rz, axis=0)


def _peer_topk(q, sk1p, sk2p):
    n, d = q.shape
    tb = min(TOPK_TB, n)
    nrow = PEER_HEADS * PEER_NKEYS
    col = lambda r: pl.BlockSpec((r, tb), lambda i: (0, i))
    return pl.pallas_call(
        _peer_topk_kernel,
        grid=(n // tb,),
        in_specs=[pl.BlockSpec((tb, d), lambda i: (i, 0)),
                  pl.BlockSpec(sk1p.shape, lambda i: (0, 0)),
                  pl.BlockSpec(sk2p.shape, lambda i: (0, 0))],
        out_specs=[col(nrow), col(nrow), col(4 * PEER_HEADS)],
        out_shape=[jax.ShapeDtypeStruct((nrow, n), F32), jax.ShapeDtypeStruct((nrow, n), F32),
                   jax.ShapeDtypeStruct((4 * PEER_HEADS, n), F32)],
        compiler_params=_params(("parallel",)),
    )(q, sk1p, sk2p)


def _peer_expert_kernel(h_ref, u_ref, vt_ref, s1_ref, s2_ref, st_ref, x_ref, o_ref,
                        acc_scr, e1_scr, e2_scr):
    j = pl.program_id(1)
    nh, nk = PEER_HEADS, PEER_NKEYS

    @pl.when(j == 0)
    def _():
        acc_scr[...] = jnp.zeros_like(acc_scr)
        for hd in range(nh):
            hs = slice(hd * nk, (hd + 1) * nk)
            e1_scr[hs, :] = jnp.exp(s1_ref[hs, :] - st_ref[nh + hd:nh + hd + 1, :])
            e2_scr[hs, :] = (jnp.exp(s2_ref[hs, :] - st_ref[2 * nh + hd:2 * nh + hd + 1, :])
                             * st_ref[3 * nh + hd:3 * nh + hd + 1, :])

    act = _dot_nt(u_ref[...], h_ref[...])
    act = 0.5 * act * (1.0 + lax.erf(act * (2.0 ** -0.5)))
    sub = u_ref.shape[0] // nk
    coef = []
    for il in range(sub):
        i1 = j * sub + il
        gate = jnp.zeros((nk, act.shape[1]), F32)
        for hd in range(nh):
            hs = slice(hd * nk, (hd + 1) * nk)
            score = s1_ref[pl.ds(hd * nk + i1, 1), :] + s2_ref[hs, :]
            weight = e1_scr[pl.ds(hd * nk + i1, 1), :] * e2_scr[hs, :]
            gate = gate + jnp.where(score >= st_ref[hd:hd + 1, :], weight, 0.0)
        coef.append((gate * act[il * nk:(il + 1) * nk]).astype(BF16))
    acc_scr[...] += jnp.dot(vt_ref[...], jnp.concatenate(coef, axis=0), preferred_element_type=F32)

    @pl.when(j == pl.num_programs(1) - 1)
    def _():
        o_ref[...] = x_ref[...] + acc_scr[...].T


def _peer_experts(h, s1, s2, st, x, u, vt):
    n, d = x.shape
    ne = u.shape[0]
    tb = min(PEER_TB, n)
    et = PEER_ET
    nrow = s1.shape[0]
    col = lambda r: pl.BlockSpec((r, tb), lambda i, j: (0, i))
    row = pl.BlockSpec((tb, d), lambda i, j: (i, 0))
    return pl.pallas_call(
        _peer_expert_kernel,
        grid=(n // tb, ne // et),
        in_specs=[row, pl.BlockSpec((et, d), lambda i, j: (j, 0)),
                  pl.BlockSpec((d, et), lambda i, j: (0, j)),
                  col(nrow), col(nrow), col(st.shape[0]), row],
        out_specs=row,
        out_shape=jax.ShapeDtypeStruct((n, d), F32),
        scratch_shapes=[pltpu.VMEM((d, tb), F32), pltpu.VMEM((nrow, tb), F32),
                        pltpu.VMEM((nrow, tb), F32)],
        compiler_params=_params(("parallel", "arbitrary")),
    )(h, u, vt, s1, s2, st, x)


def _pad_cols(a, width):
    return jnp.pad(a, ((0, 0), (0, width - a.shape[1])))


def _pad_rows(a, height):
    return jnp.pad(a, ((0, height - a.shape[0]), (0, 0)))


def _pack_rwkv_cols(a):
    o = 3 * MIX_W
    wf = a[:, o:o + RWKV_DECAY_RANK]
    wb = a[:, o + RWKV_DECAY_RANK:o + 2 * RWKV_DECAY_RANK]
    ad = a[:, o + 2 * RWKV_DECAY_RANK:o + 2 * RWKV_DECAY_RANK + RWKV_AAA_RANK]
    gd = a[:, o + 2 * RWKV_DECAY_RANK + RWKV_AAA_RANK:]
    return jnp.concatenate([a[:, :o], _pad_cols(wf, LANE), _pad_cols(wb, LANE), _pad_cols(ad, LANE),
                            _pad_cols(gd, LANE)], axis=1)


def _layer_weights(P, l, lb):
    w_in = P['w_in'][l]
    row = lambda a: a.reshape(1, -1).astype(F32)
    W = {}
    W['norm_mix_g'] = P['norm_mix_g'][l]
    W['w_gla'] = _pad_cols(w_in[:, :GLA_IN], ZGLA_W).astype(BF16)
    W['w_hgrn'] = w_in[:, GLA_IN:GLA_IN + HGRN_IN].astype(BF16)
    W['w_rwkv'] = _pack_rwkv_cols(w_in[:, GLA_IN + HGRN_IN:GATE_OFF]).astype(BF16)
    W['w_gate'] = w_in[:, GATE_OFF:].astype(BF16)
    W['gla_up_f'] = _pad_rows(P['gla_gate_up_f'][l], LANE).astype(BF16)
    W['gla_up_b'] = _pad_rows(jnp.concatenate(
        [jnp.zeros_like(P['gla_gate_up_b'][l]), P['gla_gate_up_b'][l]], axis=0), LANE).astype(BF16)
    W['gla_bias_f'] = row(P['gla_gate_bias_f'][l])
    W['gla_bias_b'] = row(P['gla_gate_bias_b'][l])
    W['gla_norm_g'] = row(P['gla_norm_g'][l])
    W['hgrn_lb'] = row(lb)
    W['hgrn_norm_g'] = row(P['hgrn_norm_g'][l])
    W['rwkv_mu_f'] = _pack_rwkv_cols(row(P['rwkv_mu_f'][l]))
    W['rwkv_mu_b'] = _pack_rwkv_cols(row(P['rwkv_mu_b'][l]))
    W['rwkv_w0_f'] = row(P['rwkv_w0_f'][l])
    W['rwkv_w2_f'] = _pad_rows(P['rwkv_w2_f'][l], LANE).astype(BF16)
    W['rwkv_w0_b'] = row(P['rwkv_w0_b'][l])
    W['rwkv_w2_b'] = _pad_rows(P['rwkv_w2_b'][l], LANE).astype(BF16)
    W['rwkv_a0'] = row(P['rwkv_a0'][l])
    W['rwkv_a2'] = _pad_rows(P['rwkv_a2'][l], LANE).astype(BF16)
    W['rwkv_g2'] = P['rwkv_g2'][l].astype(BF16)
    for name in ('rwkv_k_k', 'rwkv_k_a', 'rwkv_r_k', 'rwkv_norm_g', 'rwkv_norm_b'):
        W[name] = row(P[name][l])
    for name in ('w_branch_gla', 'w_branch_hgrn', 'w_branch_rwkv', 'w_out', 'xattn_wq', 'xattn_wo'):
        W[name] = P[name][l].astype(BF16)
    W['peer_wq'] = P['peer_wq'][l]
    W['xattn_wkv'] = jnp.concatenate([P['xattn_wk'][l], P['xattn_wv'][l]], axis=1).astype(BF16)
    for name in ('norm_x_g', 'norm_mem_g', 'norm_ffn_g'):
        W[name] = P[name][l]
    half = PEER_DK // 2
    W['peer_sk1'] = jnp.pad(P['peer_subkeys_1'][l], ((0, 0), (0, half)))
    W['peer_sk2'] = jnp.pad(P['peer_subkeys_2'][l], ((0, 0), (half, 0)))
    W['peer_u'] = P['peer_u'][l].astype(BF16)
    W['peer_vt'] = P['peer_v'][l].astype(BF16).T
    return W


def _shared_consts():
    L = RWKV_CHUNK
    idx = jnp.arange(2 * L)
    same = (idx[:, None] // L) == (idx[None, :] // L)
    tpos = idx % L
    C = {
        'gla_ebc': _head_match(GLA_W, GLA_DK, MIX_W, GLA_DV, BF16),
        'gla_bdt': _head_match(MIX_W, GLA_DV, GLA_W, GLA_DK, F32),
        'hgrn_ebc': _head_match(HGRN_W, HGRN_DK, MIX_W, HGRN_DV, BF16),
        'hgrn_bdt': _head_match(MIX_W, HGRN_DV, HGRN_W, HGRN_DK, F32),
        'hh128': _head_match(MIX_W, 128, MIX_W, 128, BF16),
        'hh64': _head_match(MIX_W, RWKV_HEAD, MIX_W, RWKV_HEAD, BF16),
        'rwkv_bd': _head_match(MIX_W, RWKV_HEAD, MIX_W, RWKV_HEAD, F32),
        'strict_f': (same & (tpos[None, :] < tpos[:, None])).astype(F32),
        'incl_f': (same & (tpos[None, :] <= tpos[:, None])).astype(F32),
        'strict_b': (same & (tpos[None, :] > tpos[:, None])).astype(F32),
        'incl_b': (same & (tpos[None, :] >= tpos[:, None])).astype(F32),
    }
    return C


def _encoder_layer(x, mem, W, C, bsz, t, n_mem):
    g_mix = W['norm_mix_g']
    z_gla = _norm_matmul(x, g_mix, W['w_gla'])
    z_hgrn = _norm_matmul(x, g_mix, W['w_hgrn'])
    z_rwkv = _norm_matmul(x, g_mix, W['w_rwkv'])
    z_gate = _norm_matmul(x, g_mix, W['w_gate'])

    gla_c = (C['gla_ebc'], C['gla_bdt'])
    o_f = _lin_attn_pass(_gla_kernel, z_gla, (W['gla_up_f'], W['gla_bias_f']) + gla_c, None, None,
                         bsz, t, GLA_W, False)
    o_gla = _lin_attn_pass(_gla_kernel, z_gla, (W['gla_up_b'], W['gla_bias_b']) + gla_c, o_f,
                           (C['hh128'], W['gla_norm_g']), bsz, t, GLA_W, True)

    hg_c = (W['hgrn_lb'], C['hgrn_ebc'], C['hgrn_bdt'])
    o_f = _lin_attn_pass(_hgrn_kernel, z_hgrn, hg_c, None, None, bsz, t, HGRN_W, False)
    o_hgrn = _lin_attn_pass(_hgrn_kernel, z_hgrn, hg_c, o_f, (C['hh128'], W['hgrn_norm_g']),
                            bsz, t, HGRN_W, True)

    prep_c = (W['rwkv_mu_f'], W['rwkv_mu_b'], W['rwkv_w0_f'], W['rwkv_w2_f'], W['rwkv_w0_b'],
              W['rwkv_w2_b'], W['rwkv_a0'], W['rwkv_a2'], W['rwkv_g2'], W['rwkv_k_k'], W['rwkv_k_a'],
              W['rwkv_r_k'], C['hh64'])
    r, kh, v, kk, b, lw_f, lw_b, g, bonus = _rwkv_prep(z_rwkv, prep_c, bsz, t)
    o_rf = _rwkv_scan(r, lw_f, kh, v, kk, b, (C['rwkv_bd'], C['strict_f'], C['incl_f']), bsz, t, False)
    o_rb = _rwkv_scan(r, lw_b, kh, v, kk, b, (C['rwkv_bd'], C['strict_b'], C['incl_b']), bsz, t, True)

    x = _merge(x, (o_gla, o_hgrn, o_rf, o_rb, bonus, g, z_gate),
               (W['w_branch_gla'], W['w_branch_hgrn'], W['w_branch_rwkv'], W['w_out'],
                W['rwkv_norm_g'], W['rwkv_norm_b'], C['hh64']))

    kv = _norm_matmul(mem, W['norm_mem_g'], W['xattn_wkv'])
    x = _xattn(x, kv, W['norm_x_g'], W['xattn_wq'], W['xattn_wo'], bsz, t, n_mem)

    q, h = _norm_matmul(x, W['norm_ffn_g'], W['peer_wq'], emit_h=True)
    s1, s2, st = _peer_topk(q, W['peer_sk1'], W['peer_sk2'])
    return _peer_experts(h, s1, s2, st, x, W['peer_u'], W['peer_vt'])


def _run_trunk(x, mem, P, weights, C):
    bsz, t, d = x.shape
    n_mem = mem.shape[1]
    x = x.reshape(bsz * t, d)
    mem = mem.reshape(bsz * n_mem, d)
    for W in weights:
        x = _encoder_layer(x, mem, W, C, bsz, t, n_mem)
    return _final_norm(x, P['final_norm_g']).reshape(bsz, t, d)


def _hgrn_lower_bounds(logits):
    sm = jax.nn.softmax(logits.astype(F32), axis=0)
    return jnp.cumsum(sm, axis=0) - sm[0]


def kernel(x_prompt, x_sample, mem_prompt, mem_sample, norm_mix_g, w_in, gla_gate_up_f, gla_gate_up_b, gla_gate_bias_f, gla_gate_bias_b, gla_norm_g, hgrn_lb_logits, hgrn_norm_g, rwkv_mu_f, rwkv_mu_b, rwkv_w0_f, rwkv_w2_f, rwkv_w0_b, rwkv_w2_b, rwkv_a0, rwkv_a2, rwkv_g2, rwkv_k_k, rwkv_k_a, rwkv_r_k, rwkv_norm_g, rwkv_norm_b, w_branch_gla, w_branch_hgrn, w_branch_rwkv, w_out, norm_x_g, norm_mem_g, xattn_wq, xattn_wk, xattn_wv, xattn_wo, norm_ffn_g, peer_wq, peer_subkeys_1, peer_subkeys_2, peer_u, peer_v, final_norm_g):
    P = dict(norm_mix_g=norm_mix_g, w_in=w_in, gla_gate_up_f=gla_gate_up_f, gla_gate_up_b=gla_gate_up_b,
             gla_gate_bias_f=gla_gate_bias_f, gla_gate_bias_b=gla_gate_bias_b, gla_norm_g=gla_norm_g,
             hgrn_lb_logits=hgrn_lb_logits, hgrn_norm_g=hgrn_norm_g, rwkv_mu_f=rwkv_mu_f, rwkv_mu_b=rwkv_mu_b,
             rwkv_w0_f=rwkv_w0_f, rwkv_w2_f=rwkv_w2_f, rwkv_w0_b=rwkv_w0_b, rwkv_w2_b=rwkv_w2_b,
             rwkv_a0=rwkv_a0, rwkv_a2=rwkv_a2, rwkv_g2=rwkv_g2, rwkv_k_k=rwkv_k_k, rwkv_k_a=rwkv_k_a,
             rwkv_r_k=rwkv_r_k, rwkv_norm_g=rwkv_norm_g, rwkv_norm_b=rwkv_norm_b, w_branch_gla=w_branch_gla,
             w_branch_hgrn=w_branch_hgrn, w_branch_rwkv=w_branch_rwkv, w_out=w_out, norm_x_g=norm_x_g,
             norm_mem_g=norm_mem_g, xattn_wq=xattn_wq, xattn_wk=xattn_wk, xattn_wv=xattn_wv, xattn_wo=xattn_wo,
             norm_ffn_g=norm_ffn_g, peer_wq=peer_wq, peer_subkeys_1=peer_subkeys_1,
             peer_subkeys_2=peer_subkeys_2, peer_u=peer_u, peer_v=peer_v, final_norm_g=final_norm_g)
    depth = w_in.shape[0]
    lbs = _hgrn_lower_bounds(hgrn_lb_logits)
    weights = [_layer_weights(P, l, lbs[l]) for l in range(depth)]
    C = _shared_consts()
    return (_run_trunk(x_prompt, mem_prompt, P, weights, C),
            _run_trunk(x_sample, mem_sample, P, weights, C))
```

```python
import functools

import jax
import jax.numpy as jnp
from jax import lax
from jax.experimental import pallas as pl
from jax.experimental.pallas import tpu as pltpu

F32 = jnp.float32
BF16 = jnp.bfloat16

D_MODEL = 1024
EPS = 1e-6
LOG_FLOOR = 1e-30
MIX_W = 512
GLA_HEADS, GLA_DK, GLA_DV = 4, 64, 128
GLA_GATE_RANK = 16
GLA_GATE_NORM = 16.0
HGRN_HEADS, HGRN_DK, HGRN_DV = 4, 128, 128
RWKV_HEADS, RWKV_HEAD = 8, 64
RWKV_DECAY_RANK, RWKV_AAA_RANK, RWKV_GATE_RANK = 64, 64, 128
RWKV_DECAY_SCALE = 0.606531
RWKV_GN_EPS = 64e-5
X_HEADS = 4
X_HEAD = D_MODEL // X_HEADS
PEER_HEADS, PEER_DK, PEER_NKEYS, PEER_TOPK = 8, 128, 128, 16

GLA_W = GLA_HEADS * GLA_DK
GLA_IN = 2 * GLA_W + 2 * MIX_W + 2 * GLA_GATE_RANK
HGRN_W = HGRN_HEADS * HGRN_DK
HGRN_IN = 5 * MIX_W
RWKV_IN = 3 * MIX_W + 2 * RWKV_DECAY_RANK + RWKV_AAA_RANK + RWKV_GATE_RANK
GATE_OFF = GLA_IN + HGRN_IN + RWKV_IN

LANE = 128
SUBLANES = 8
ZGLA_W = 2 * GLA_W + 2 * MIX_W + LANE
ZRWKV_W = 3 * MIX_W + 4 * LANE

ROW_TILE = 512
MERGE_TILE = 256
GLA_CHUNK = 32
GLA_TB = 256
RWKV_CHUNK = 64
RWKV_TB = 256
PREP_TB = 256
XATTN_TQ = 256
TOPK_TB = 128
PEER_TB = 256
PEER_ET = 1024
VMEM_LIMIT = 48 * 1024 * 1024


def _params(sem):
    return pltpu.CompilerParams(dimension_semantics=sem, vmem_limit_bytes=VMEM_LIMIT)


def _dot(a, b):
    return jnp.dot(a.astype(BF16), b.astype(BF16), preferred_element_type=F32)


def _dot_nt(a, b):
    return lax.dot_general(a.astype(BF16), b.astype(BF16), (((1,), (1,)), ((), ())),
                           preferred_element_type=F32)


def _dot_nt_f32(a, b):
    return lax.dot_general(a, b, (((1,), (1,)), ((), ())), precision=lax.Precision.HIGHEST,
                           preferred_element_type=F32)


def _dot_tn(a, b):
    return lax.dot_general(a.astype(BF16), b.astype(BF16), (((0,), (0,)), ((), ())),
                           preferred_element_type=F32)


def _split3(x):
    hi = x.astype(BF16)
    r1 = x - hi.astype(F32)
    mid = r1.astype(BF16)
    lo = (r1 - mid.astype(F32)).astype(BF16)
    return hi, mid, lo


def _dot_exact_rhs(x, m):
    hi, mid, lo = _split3(x)
    f = lambda p: jnp.dot(p, m, preferred_element_type=F32)
    return f(hi) + f(mid) + f(lo)


def _dot_exact_lhs(m, x):
    hi, mid, lo = _split3(x)
    f = lambda p: jnp.dot(m, p, preferred_element_type=F32)
    return f(hi) + f(mid) + f(lo)


def _tri_incl(n):
    r = lax.broadcasted_iota(jnp.int32, (n, n), 0)
    c = lax.broadcasted_iota(jnp.int32, (n, n), 1)
    return (r >= c).astype(BF16)


def _rmsnorm(x, g):
    return x * lax.rsqrt(jnp.mean(x * x, axis=-1, keepdims=True) + EPS) * g


def _log_sigmoid(x):
    return jnp.minimum(x, 0.0) - jnp.log(1.0 + jnp.exp(-jnp.abs(x)))


def _silu(x):
    return x * jax.nn.sigmoid(x)


def _norm_matmul_kernel(x_ref, g_ref, w_ref, o_ref, *h_ref):
    h = _rmsnorm(x_ref[...], g_ref[...])
    if w_ref.dtype == F32:
        o_ref[...] = jnp.dot(h, w_ref[...], precision=lax.Precision.HIGHEST, preferred_element_type=F32)
    else:
        o_ref[...] = jnp.dot(h.astype(BF16), w_ref[...], preferred_element_type=F32)
    if h_ref:
        h_ref[0][...] = h.astype(h_ref[0].dtype)


def _norm_matmul(x, g, w, emit_h=False):
    n, d = x.shape
    c = w.shape[1]
    tm = min(ROW_TILE, n)
    out_shape = [jax.ShapeDtypeStruct((n, c), F32)]
    out_specs = [pl.BlockSpec((tm, c), lambda i: (i, 0))]
    if emit_h:
        out_shape.append(jax.ShapeDtypeStruct((n, d), BF16))
        out_specs.append(pl.BlockSpec((tm, d), lambda i: (i, 0)))
    res = pl.pallas_call(
        _norm_matmul_kernel,
        grid=(n // tm,),
        in_specs=[pl.BlockSpec((tm, d), lambda i: (i, 0)),
                  pl.BlockSpec((1, d), lambda i: (0, 0)),
                  pl.BlockSpec((d, c), lambda i: (0, 0))],
        out_specs=out_specs,
        out_shape=out_shape,
        compiler_params=_params(("parallel",)),
    )(x, g.reshape(1, d), w)
    return res if emit_h else res[0]


def _final_norm_kernel(x_ref, g_ref, o_ref):
    o_ref[...] = _rmsnorm(x_ref[...], g_ref[...])


def _final_norm(x, g):
    n, d = x.shape
    tm = min(ROW_TILE, n)
    return pl.pallas_call(
        _final_norm_kernel,
        grid=(n // tm,),
        in_specs=[pl.BlockSpec((tm, d), lambda i: (i, 0)), pl.BlockSpec((1, d), lambda i: (0, 0))],
        out_specs=pl.BlockSpec((tm, d), lambda i: (i, 0)),
        out_shape=jax.ShapeDtypeStruct((n, d), F32),
        compiler_params=_params(("parallel",)),
    )(x, g.reshape(1, d))


def _gla_time_block(q, k, v, g, o_scr, st_scr, qs, ks, vs, cs, cp, ebc_ref, bdt_ref, tri_ref, blk_ref,
                    reverse):
    tb, w = q.shape
    hv = v.shape[1]
    C = GLA_CHUNK
    nchunks = tb // C
    cum = _dot_exact_lhs(tri_ref[...], g)
    tot = _dot_exact_lhs(blk_ref[...], g)
    if reverse:
        cum = tot - cum + g
    qs[...] = q
    cs[...] = cum
    pads = [jnp.zeros((C, w), F32), jnp.zeros((C, w), F32), jnp.zeros((C, hv), F32)]
    for dst, val, pad in zip((ks, cp, vs), (k, cum, v), pads):
        ext = jnp.concatenate([val, pad] if reverse else [pad, val], axis=0)
        for b in range(SUBLANES):
            shift = ((tb + C - b) % (tb + C)) if reverse else b
            dst[b] = pltpu.roll(ext, shift, 0) if shift else ext
    pos = lax.broadcasted_iota(jnp.int32, (tb, 1), 0) % C
    dk = w // (hv // LANE)
    for h in range(hv // LANE):
        lt = (h * dk) // LANE
        ls = slice(lt * LANE, (lt + 1) * LANE)
        hs = slice(h * LANE, (h + 1) * LANE)

        def body(a, o, h=h, ls=ls, hs=hs):
            a8 = a * SUBLANES
            off = pl.multiple_of(a8 if reverse else C - a8, SUBLANES)
            for b in range(SUBLANES):
                d = a8 + b
                kr = ks[b, pl.ds(off, tb), ls]
                cr = cp[b, pl.ds(off, tb), ls]
                vr = vs[b, pl.ds(off, tb), hs]
                live = (pos <= C - 1 - d) if reverse else (pos >= d)
                w_ij = qs[:, ls] * kr * jnp.exp(jnp.minimum(cs[:, ls] - cr, 0.0))
                w_ij = jnp.where(live, w_ij, 0.0).astype(BF16)
                o = o + jnp.dot(w_ij, ebc_ref[h], preferred_element_type=F32) * vr
            return o

        o_scr[:, hs] = lax.fori_loop(0, C // SUBLANES, body, jnp.zeros((tb, LANE), F32))
    qe = q * jnp.exp(cum)
    kd = k * jnp.exp(tot - cum)
    dec = jnp.exp(tot)
    order = range(nchunks - 1, -1, -1) if reverse else range(nchunks)
    for c in order:
        sl = slice(c * GLA_CHUNK, (c + 1) * GLA_CHUNK)
        st = st_scr[...]
        o_scr[sl, :] = o_scr[sl, :] + _dot_nt(qe[sl], st)
        st_scr[...] = (st * dec[c * GLA_CHUNK:c * GLA_CHUNK + 1]
                       + _dot_tn(v[sl], kd[sl]) * bdt_ref[...])


def _head_rms_gate(o, og, hh_ref, ng_ref, dv):
    ms = _dot_exact_rhs(o * o, hh_ref[...]) * (1.0 / dv)
    return o * lax.rsqrt(ms + EPS) * ng_ref[...] * _silu(og)


def _gla_kernel(*refs, reverse, final):
    if final:
        (z_ref, up_ref, bias_ref, ebc_ref, bdt_ref, tri_ref, blk_ref, oprev_ref, hh_ref, ng_ref,
         o_ref, st_scr, o_scr, qs, ks, vs, cs, cp) = refs
    else:
        (z_ref, up_ref, bias_ref, ebc_ref, bdt_ref, tri_ref, blk_ref,
         o_ref, st_scr, o_scr, qs, ks, vs, cs, cp) = refs

    @pl.when(pl.program_id(1) == 0)
    def _():
        st_scr[...] = jnp.zeros_like(st_scr)

    z = z_ref[...]
    q = z[:, 0:GLA_W] * (GLA_DK ** -0.5)
    k = z[:, GLA_W:2 * GLA_W]
    v = z[:, 2 * GLA_W:2 * GLA_W + MIX_W]
    gd = z[:, 2 * GLA_W + 2 * MIX_W:ZGLA_W]
    g = _log_sigmoid(_dot(gd, up_ref[...]) + bias_ref[...]) * (1.0 / GLA_GATE_NORM)
    _gla_time_block(q, k, v, g, o_scr, st_scr, qs, ks, vs, cs, cp, ebc_ref, bdt_ref, tri_ref, blk_ref, reverse)
    if final:
        og = z[:, 2 * GLA_W + MIX_W:2 * GLA_W + 2 * MIX_W]
        o_ref[...] = _head_rms_gate(oprev_ref[...] + o_scr[...], og, hh_ref, ng_ref, GLA_DV)
    else:
        o_ref[...] = o_scr[...]


def _hgrn_kernel(*refs, reverse, final):
    if final:
        (z_ref, lb_ref, ebc_ref, bdt_ref, tri_ref, blk_ref, oprev_ref, hh_ref, ng_ref,
         o_ref, st_scr, o_scr, qs, ks, vs, cs, cp) = refs
    else:
        z_ref, lb_ref, ebc_ref, bdt_ref, tri_ref, blk_ref, o_ref, st_scr, o_scr, qs, ks, vs, cs, cp = refs

    @pl.when(pl.program_id(1) == 0)
    def _():
        st_scr[...] = jnp.zeros_like(st_scr)

    z = z_ref[...]
    lb = lb_ref[...]
    q = _silu(z[:, 0:HGRN_W])
    zf = z[:, (2 if reverse else 1) * HGRN_W:(3 if reverse else 2) * HGRN_W]
    v = z[:, 3 * HGRN_W:4 * HGRN_W]
    f = lb + (1.0 - lb) * jax.nn.sigmoid(zf)
    g = jnp.log(jnp.maximum(f, LOG_FLOOR))
    k = (1.0 - lb) * jax.nn.sigmoid(-zf)
    _gla_time_block(q, k, v, g, o_scr, st_scr, qs, ks, vs, cs, cp, ebc_ref, bdt_ref, tri_ref, blk_ref, reverse)
    if final:
        og = z[:, 4 * HGRN_W:5 * HGRN_W]
        o_ref[...] = _head_rms_gate(oprev_ref[...] + o_scr[...], og, hh_ref, ng_ref, HGRN_DV)
    else:
        o_ref[...] = o_scr[...]


def _seq_row_map(nblk, reverse):
    if reverse:
        return lambda b, i: (b * nblk + nblk - 1 - i, 0)
    return lambda b, i: (b * nblk + i, 0)


def _const_map(b, i):
    return (0, 0)


def _lin_attn_pass(kernel, z, consts, o_prev, final_consts, bsz, t, w, reverse):
    n, zc = z.shape
    tb = min(GLA_TB, t)
    nblk = t // tb
    rmap = _seq_row_map(nblk, reverse)
    final = o_prev is not None
    r = jnp.arange(tb)
    same = (r[:, None] // GLA_CHUNK) == (r[None, :] // GLA_CHUNK)
    consts = tuple(consts) + ((same & (r[None, :] <= r[:, None])).astype(BF16), same.astype(BF16))
    args = [z] + list(consts)
    in_specs = [pl.BlockSpec((tb, zc), rmap)]
    in_specs += [pl.BlockSpec(c.shape, lambda b, i, nd=c.ndim: (0,) * nd) for c in consts]
    if final:
        args += [o_prev] + list(final_consts)
        in_specs += [pl.BlockSpec((tb, MIX_W), rmap)]
        in_specs += [pl.BlockSpec(c.shape, _const_map) for c in final_consts]
    return pl.pallas_call(
        functools.partial(kernel, reverse=reverse, final=final),
        grid=(bsz, nblk),
        in_specs=in_specs,
        out_specs=pl.BlockSpec((tb, MIX_W), rmap),
        out_shape=jax.ShapeDtypeStruct((n, MIX_W), F32),
        scratch_shapes=[pltpu.VMEM((MIX_W, w), F32), pltpu.VMEM((tb, MIX_W), F32),
                        pltpu.VMEM((tb, w), F32), pltpu.VMEM((SUBLANES, tb + GLA_CHUNK, w), F32),
                        pltpu.VMEM((SUBLANES, tb + GLA_CHUNK, MIX_W), F32), pltpu.VMEM((tb, w), F32),
                        pltpu.VMEM((SUBLANES, tb + GLA_CHUNK, w), F32)],
        compiler_params=_params(("parallel", "arbitrary")),
    )(*args)


def _head_select(dk, heads):
    h = jnp.arange(heads)[:, None]
    lane = ((h * dk) // LANE) * LANE + jnp.arange(LANE)[None, :]
    own = (lane // dk == h).astype(BF16)
    return jnp.broadcast_to(own[:, :, None], (heads, LANE, LANE))


def _head_match(n_rows, row_blk, n_cols, col_blk, dtype):
    r = jnp.arange(n_rows) // row_blk
    c = jnp.arange(n_cols) // col_blk
    return (r[:, None] == c[None, :]).astype(dtype)


def _rwkv_prep_kernel(z_ref, zp_ref, zn_ref, muf_ref, mub_ref, w0f_ref, w2f_ref, w0b_ref, w2b_ref,
                      a0_ref, a2_ref, g2_ref, kk_ref, ka_ref, rk_ref, hh_ref,
                      r_out, kh_out, v_out, kkn_out, b_out, lwf_out, lwb_out, g_out, bonus_out):
    i = pl.program_id(1)
    last = pl.num_programs(1) - 1
    z = z_ref[...]
    tb = z.shape[0]
    rows = lax.broadcasted_iota(jnp.int32, (tb, 1), 0)
    hp = jnp.where(i == 0, 0.0, zp_ref[7:8, :])
    hn = jnp.where(i == last, 0.0, zn_ref[0:1, :])
    prev = jnp.where(rows == 0, hp, pltpu.roll(z, 1, 0))
    nxt = jnp.where(rows == tb - 1, hn, pltpu.roll(z, tb - 1, 0))
    p = z + muf_ref[...] * (prev - z) + mub_ref[...] * (nxt - z)
    r = p[:, 0:MIX_W]
    k = p[:, MIX_W:2 * MIX_W]
    v = p[:, 2 * MIX_W:3 * MIX_W]
    o = 3 * MIX_W
    wdf, wdb, ad, gd = (p[:, o + j * LANE:o + (j + 1) * LANE] for j in range(4))
    lwf = -RWKV_DECAY_SCALE * jax.nn.sigmoid(w0f_ref[...] + _dot(jnp.tanh(wdf), w2f_ref[...]))
    lwb = -RWKV_DECAY_SCALE * jax.nn.sigmoid(w0b_ref[...] + _dot(jnp.tanh(wdb), w2b_ref[...]))
    a = jax.nn.sigmoid(a0_ref[...] + _dot(ad, a2_ref[...]))
    g = _dot(jax.nn.sigmoid(gd), g2_ref[...])
    kk = k * kk_ref[...]
    ss = _dot_exact_rhs(kk * kk, hh_ref[...])
    kk = kk / jnp.maximum(jnp.sqrt(ss), 1e-12)
    kh = k * (1.0 + (a - 1.0) * ka_ref[...])
    bonus = _dot_exact_rhs(r * kh * rk_ref[...], hh_ref[...]) * v
    r_out[...] = r
    kh_out[...] = kh
    v_out[...] = v
    kkn_out[...] = kk
    b_out[...] = kk * a
    lwf_out[...] = lwf
    lwb_out[...] = lwb
    g_out[...] = g
    bonus_out[...] = bonus


def _rwkv_prep(z, consts, bsz, t):
    n, zc = z.shape
    tb = min(PREP_TB, t)
    nblk = t // tb
    hb = tb // 8
    nrow8 = n // 8
    rmap = _seq_row_map(nblk, False)
    pmap = lambda b, i: (jnp.maximum((b * nblk + i) * hb - 1, 0), 0)
    nmap = lambda b, i: (jnp.minimum((b * nblk + i + 1) * hb, nrow8 - 1), 0)
    in_specs = [pl.BlockSpec((tb, zc), rmap), pl.BlockSpec((8, zc), pmap), pl.BlockSpec((8, zc), nmap)]
    in_specs += [pl.BlockSpec(c.shape, _const_map) for c in consts]
    return pl.pallas_call(
        _rwkv_prep_kernel,
        grid=(bsz, nblk),
        in_specs=in_specs,
        out_specs=[pl.BlockSpec((tb, MIX_W), rmap)] * 9,
        out_shape=[jax.ShapeDtypeStruct((n, MIX_W), F32)] * 9,
        compiler_params=_params(("parallel", "parallel")),
    )(z, z, z, *consts)


def _rwkv_scan_kernel(r_ref, lw_ref, kh_ref, v_ref, kk_ref, b_ref, bd_ref, mstrict_ref, mincl_ref,
                      o_ref, s_scr, *, reverse):
    @pl.when(pl.program_id(1) == 0)
    def _():
        s_scr[...] = jnp.zeros_like(s_scr)

    L = RWKV_CHUNK
    tb = r_ref.shape[0]
    nchunks = tb // L
    tri = _tri_incl(L)
    lane = lax.broadcasted_iota(jnp.int32, (1, LANE), 1)
    m0 = (lane < RWKV_HEAD).astype(F32)
    m1 = 1.0 - m0
    eye = (lax.broadcasted_iota(jnp.int32, (2 * L, 2 * L), 0)
           == lax.broadcasted_iota(jnp.int32, (2 * L, 2 * L), 1)).astype(F32)
    mstrict = mstrict_ref[...]
    mincl = mincl_ref[...]
    groups = range(MIX_W // LANE)
    lanes = lambda p: slice(p * LANE, (p + 1) * LANE)
    stack2 = lambda x: jnp.concatenate([x * m0, x * m1], axis=0)

    pre = []
    for c in range(nchunks):
        sl = slice(c * L, (c + 1) * L)
        r, lw, kh, v, kk, b = (ref[sl, :] for ref in (r_ref, lw_ref, kh_ref, v_ref, kk_ref, b_ref))
        cum = _dot_exact_lhs(tri, lw)
        tot = cum[L - 1:L]
        inc = (tot - cum + lw) if reverse else cum
        exc = inc - lw
        mid = inc[L // 2:L // 2 + 1]
        e_pos = jnp.exp(inc - mid)
        e_neg = jnp.exp(mid - inc)
        e_end = jnp.exp(tot - inc)
        pre.append(dict(sl=sl, v=v, rt=r * e_pos, at=-kk * jnp.exp(exc - mid), bt=b * e_neg, kt=kh * e_neg,
                        a_abs=-kk * jnp.exp(exc), r_abs=r * jnp.exp(inc), dec=jnp.exp(tot),
                        wr=jnp.concatenate([b * e_end, kh * e_end], axis=0)))
    cells = [(c, p) for c in range(nchunks) for p in groups]
    gram = {}
    for c, p in cells:
        d = pre[c]
        lhs = jnp.concatenate([stack2(d['at'][:, lanes(p)]), stack2(d['rt'][:, lanes(p)])], axis=0)
        b_p, k_p = d['bt'][:, lanes(p)], d['kt'][:, lanes(p)]
        gram[c, p] = _dot_nt(lhs, jnp.concatenate([b_p, b_p, k_p, k_p], axis=0))
    n_ab = {k: g[0:2 * L, 0:2 * L] * mstrict for k, g in gram.items()}
    a_ak = {k: g[0:2 * L, 2 * L:4 * L] * mstrict for k, g in gram.items()}
    a_rb = {k: g[2 * L:4 * L, 0:2 * L] * mincl for k, g in gram.items()}
    a_rk = {k: g[2 * L:4 * L, 2 * L:4 * L] * mincl for k, g in gram.items()}
    v_bd = {(c, p): stack2(pre[c]['v'][:, lanes(p)]) for c, p in cells}
    x0 = {k: _dot(a_ak[k], v_bd[k]) for k in cells}
    o0 = {k: _dot(a_rk[k], v_bd[k]) for k in cells}
    tinv = {k: eye + n_ab[k] for k in cells}
    pw = dict(n_ab)
    for _ in range(5):
        pw = {k: _dot(pw[k], pw[k]) for k in cells}
        tinv = {k: tinv[k] + _dot(pw[k], tinv[k]) for k in cells}

    fold2 = lambda x: x[0:L] + x[L:2 * L]
    for c in (range(nchunks - 1, -1, -1) if reverse else range(nchunks)):
        d = pre[c]
        s = s_scr[...]
        am = _dot_nt(d['a_abs'], s)
        rm = _dot_nt(d['r_abs'], s)
        us = [_dot(tinv[c, p], stack2(am[:, lanes(p)]) + x0[c, p]) for p in groups]
        o_bd = [_dot(a_rb[c, p], us[p]) + o0[c, p] for p in groups]
        o_ref[d['sl'], :] = rm + jnp.concatenate([fold2(o) for o in o_bd], axis=1)
        u_all = jnp.concatenate([fold2(u) for u in us], axis=1)
        upd = _dot_tn(jnp.concatenate([u_all, d['v']], axis=0), d['wr'])
        s_scr[...] = s * d['dec'] + upd * bd_ref[...]


def _rwkv_scan(r, lw, kh, v, kk, b, consts, bsz, t, reverse):
    n = r.shape[0]
    tb = min(RWKV_TB, t)
    nblk = t // tb
    rmap = _seq_row_map(nblk, reverse)
    row_spec = pl.BlockSpec((tb, MIX_W), rmap)
    return pl.pallas_call(
        functools.partial(_rwkv_scan_kernel, reverse=reverse),
        grid=(bsz, nblk),
        in_specs=[row_spec] * 6 + [pl.BlockSpec(c.shape, _const_map) for c in consts],
        out_specs=row_spec,
        out_shape=jax.ShapeDtypeStruct((n, MIX_W), F32),
        scratch_shapes=[pltpu.VMEM((MIX_W, MIX_W), F32)],
        compiler_params=_params(("parallel", "arbitrary")),
    )(r, lw, kh, v, kk, b, *consts)


def _merge_kernel(x_ref, ogla_ref, ohg_ref, orf_ref, orb_ref, bonus_ref, g_ref, zg_ref,
                  wg_ref, wh_ref, wr_ref, wo_ref, ng_ref, nb_ref, hh_ref, o_ref):
    o = orf_ref[...] + orb_ref[...]
    inv = 1.0 / RWKV_HEAD
    mean = _dot_exact_rhs(o, hh_ref[...]) * inv
    d = o - mean
    var = _dot_exact_rhs(d * d, hh_ref[...]) * inv
    on = d * lax.rsqrt(var + RWKV_GN_EPS) * ng_ref[...] + nb_ref[...]
    orw = (on + bonus_ref[...]) * g_ref[...]
    zg = zg_ref[...]
    gate = lambda j: jax.nn.sigmoid(zg[:, j * D_MODEL:(j + 1) * D_MODEL])
    merged = (gate(0) * _dot(ogla_ref[...], wg_ref[...])
              + gate(1) * _dot(ohg_ref[...], wh_ref[...])
              + gate(2) * _dot(orw, wr_ref[...]))
    o_ref[...] = x_ref[...] + _dot(merged, wo_ref[...])


def _merge(x, rows, consts):
    n, d = x.shape
    tm = min(MERGE_TILE, n)
    rspec = lambda a: pl.BlockSpec((tm, a.shape[1]), lambda i: (i, 0))
    return pl.pallas_call(
        _merge_kernel,
        grid=(n // tm,),
        in_specs=[rspec(x)] + [rspec(a) for a in rows]
        + [pl.BlockSpec(c.shape, lambda i: (0, 0)) for c in consts],
        out_specs=pl.BlockSpec((tm, d), lambda i: (i, 0)),
        out_shape=jax.ShapeDtypeStruct((n, d), F32),
        compiler_params=_params(("parallel",)),
    )(x, *rows, *consts)


def _xattn_kernel(x_ref, kv_ref, g_ref, wq_ref, wo_ref, o_ref):
    x = x_ref[...]
    h = _rmsnorm(x, g_ref[...])
    q = _dot(h, wq_ref[...])
    kv = kv_ref[...]
    outs = []
    for hd in range(X_HEADS):
        ls = slice(hd * X_HEAD, (hd + 1) * X_HEAD)
        s = _dot_nt(q[:, ls], kv[:, ls]) * (X_HEAD ** -0.5)
        s = s - jnp.max(s, axis=-1, keepdims=True)
        e = jnp.exp(s)
        pr = e / jnp.sum(e, axis=-1, keepdims=True)
        outs.append(_dot(pr, kv[:, D_MODEL + hd * X_HEAD:D_MODEL + (hd + 1) * X_HEAD]))
    o_ref[...] = x + _dot(jnp.concatenate(outs, axis=1), wo_ref[...])


def _xattn(x, kv, g, wq, wo, bsz, t, n_mem):
    n, d = x.shape
    tq = min(XATTN_TQ, t)
    nblk = t // tq
    return pl.pallas_call(
        _xattn_kernel,
        grid=(bsz, nblk),
        in_specs=[pl.BlockSpec((tq, d), lambda b, i: (b * nblk + i, 0)),
                  pl.BlockSpec((n_mem, 2 * d), lambda b, i: (b, 0)),
                  pl.BlockSpec((1, d), _const_map),
                  pl.BlockSpec((d, d), _const_map),
                  pl.BlockSpec((d, d), _const_map)],
        out_specs=pl.BlockSpec((tq, d), lambda b, i: (b * nblk + i, 0)),
        out_shape=jax.ShapeDtypeStruct((n, d), F32),
        compiler_params=_params(("parallel", "parallel")),
    )(x, kv, g.reshape(1, d), wq, wo)


def _top16_rows(x):
    nrow = x.shape[0]
    rows = lax.broadcasted_iota(jnp.int32, x.shape, 0)
    vals = []
    for _ in range(PEER_TOPK):
        m = jnp.max(x, axis=0, keepdims=True)
        pos = jnp.min(jnp.where(x == m, rows, nrow), axis=0, keepdims=True)
        vals.append(m)
        x = jnp.where(rows == pos, -jnp.inf, x)
    return jnp.concatenate(vals, 0)


def _peer_topk_kernel(q_ref, sk1_ref, sk2_ref, s1_ref, s2_ref, st_ref):
    q = q_ref[...]
    thr, mx1, mx2, rz = [], [], [], []
    for hd in range(PEER_HEADS):
        hs = slice(hd * PEER_NKEYS, (hd + 1) * PEER_NKEYS)
        qh = q[:, hd * PEER_DK:(hd + 1) * PEER_DK]
        s1 = _dot_nt_f32(sk1_ref[...], qh)
        s2 = _dot_nt_f32(sk2_ref[...], qh)
        s1_ref[hs, :] = s1
        s2_ref[hs, :] = s2
        v1 = _top16_rows(s1)
        v2 = _top16_rows(s2)
        cand = jnp.concatenate([v1[a:a + 1] + v2 for a in range(PEER_TOPK)], axis=0)
        top = _top16_rows(cand)
        thr.append(top[PEER_TOPK - 1:PEER_TOPK])
        mx1.append(v1[0:1])
        mx2.append(v2[0:1])
        rz.append(1.0 / jnp.sum(jnp.exp(top - top[0:1]), axis=0, keepdims=True))
    st_ref[...] = jnp.concatenate(thr + mx1 + mx2 + rz, axis=0)


def _peer_topk(q, sk1p, sk2p):
    n, d = q.shape
    tb = min(TOPK_TB, n)
    nrow = PEER_HEADS * PEER_NKEYS
    col = lambda r: pl.BlockSpec((r, tb), lambda i: (0, i))
    return pl.pallas_call(
        _peer_topk_kernel,
        grid=(n // tb,),
        in_specs=[pl.BlockSpec((tb, d), lambda i: (i, 0)),
                  pl.BlockSpec(sk1p.shape, lambda i: (0, 0)),
                  pl.BlockSpec(sk2p.shape, lambda i: (0, 0))],
        out_specs=[col(nrow), col(nrow), col(4 * PEER_HEADS)],
        out_shape=[jax.ShapeDtypeStruct((nrow, n), F32), jax.ShapeDtypeStruct((nrow, n), F32),
                   jax.ShapeDtypeStruct((4 * PEER_HEADS, n), F32)],
        compiler_params=_params(("parallel",)),
    )(q, sk1p, sk2p)


def _peer_expert_kernel(h_ref, u_ref, vt_ref, s1_ref, s2_ref, st_ref, x_ref, o_ref,
                        acc_scr, e1_scr, e2_scr):
    j = pl.program_id(1)
    nh, nk = PEER_HEADS, PEER_NKEYS

    @pl.when(j == 0)
    def _():
        acc_scr[...] = jnp.zeros_like(acc_scr)
        for hd in range(nh):
            hs = slice(hd * nk, (hd + 1) * nk)
            e1_scr[hs, :] = jnp.exp(s1_ref[hs, :] - st_ref[nh + hd:nh + hd + 1, :])
            e2_scr[hs, :] = (jnp.exp(s2_ref[hs, :] - st_ref[2 * nh + hd:2 * nh + hd + 1, :])
                             * st_ref[3 * nh + hd:3 * nh + hd + 1, :])

    act = _dot_nt(u_ref[...], h_ref[...])
    act = 0.5 * act * (1.0 + lax.erf(act * (2.0 ** -0.5)))
    sub = u_ref.shape[0] // nk
    coef = []
    for il in range(sub):
        i1 = j * sub + il
        gate = jnp.zeros((nk, act.shape[1]), F32)
        for hd in range(nh):
            hs = slice(hd * nk, (hd + 1) * nk)
            score = s1_ref[pl.ds(hd * nk + i1, 1), :] + s2_ref[hs, :]
            weight = e1_scr[pl.ds(hd * nk + i1, 1), :] * e2_scr[hs, :]
            gate = gate + jnp.where(score >= st_ref[hd:hd + 1, :], weight, 0.0)
        coef.append((gate * act[il * nk:(il + 1) * nk]).astype(BF16))
    acc_scr[...] += jnp.dot(vt_ref[...], jnp.concatenate(coef, axis=0), preferred_element_type=F32)

    @pl.when(j == pl.num_programs(1) - 1)
    def _():
        o_ref[...] = x_ref[...] + acc_scr[...].T


def _peer_experts(h, s1, s2, st, x, u, vt):
    n, d = x.shape
    ne = u.shape[0]
    tb = min(PEER_TB, n)
    et = PEER_ET
    nrow = s1.shape[0]
    col = lambda r: pl.BlockSpec((r, tb), lambda i, j: (0, i))
    row = pl.BlockSpec((tb, d), lambda i, j: (i, 0))
    return pl.pallas_call(
        _peer_expert_kernel,
        grid=(n // tb, ne // et),
        in_specs=[row, pl.BlockSpec((et, d), lambda i, j: (j, 0)),
                  pl.BlockSpec((d, et), lambda i, j: (0, j)),
                  col(nrow), col(nrow), col(st.shape[0]), row],
        out_specs=row,
        out_shape=jax.ShapeDtypeStruct((n, d), F32),
        scratch_shapes=[pltpu.VMEM((d, tb), F32), pltpu.VMEM((nrow, tb), F32),
                        pltpu.VMEM((nrow, tb), F32)],
        compiler_params=_params(("parallel", "arbitrary")),
    )(h, u, vt, s1, s2, st, x)


def _pad_cols(a, width):
    return jnp.pad(a, ((0, 0), (0, width - a.shape[1])))


def _pad_rows(a, height):
    return jnp.pad(a, ((0, height - a.shape[0]), (0, 0)))


def _pack_rwkv_cols(a):
    o = 3 * MIX_W
    wf = a[:, o:o + RWKV_DECAY_RANK]
    wb = a[:, o + RWKV_DECAY_RANK:o + 2 * RWKV_DECAY_RANK]
    ad = a[:, o + 2 * RWKV_DECAY_RANK:o + 2 * RWKV_DECAY_RANK + RWKV_AAA_RANK]
    gd = a[:, o + 2 * RWKV_DECAY_RANK + RWKV_AAA_RANK:]
    return jnp.concatenate([a[:, :o], _pad_cols(wf, LANE), _pad_cols(wb, LANE), _pad_cols(ad, LANE),
                            _pad_cols(gd, LANE)], axis=1)


def _layer_weights(P, l, lb):
    w_in = P['w_in'][l]
    row = lambda a: a.reshape(1, -1).astype(F32)
    W = {}
    W['norm_mix_g'] = P['norm_mix_g'][l]
    W['w_gla'] = _pad_cols(w_in[:, :GLA_IN], ZGLA_W).astype(BF16)
    W['w_hgrn'] = w_in[:, GLA_IN:GLA_IN + HGRN_IN].astype(BF16)
    W['w_rwkv'] = _pack_rwkv_cols(w_in[:, GLA_IN + HGRN_IN:GATE_OFF]).astype(BF16)
    W['w_gate'] = w_in[:, GATE_OFF:].astype(BF16)
    W['gla_up_f'] = _pad_rows(P['gla_gate_up_f'][l], LANE).astype(BF16)
    W['gla_up_b'] = _pad_rows(jnp.concatenate(
        [jnp.zeros_like(P['gla_gate_up_b'][l]), P['gla_gate_up_b'][l]], axis=0), LANE).astype(BF16)
    W['gla_bias_f'] = row(P['gla_gate_bias_f'][l])
    W['gla_bias_b'] = row(P['gla_gate_bias_b'][l])
    W['gla_norm_g'] = row(P['gla_norm_g'][l])
    W['hgrn_lb'] = row(lb)
    W['hgrn_norm_g'] = row(P['hgrn_norm_g'][l])
    W['rwkv_mu_f'] = _pack_rwkv_cols(row(P['rwkv_mu_f'][l]))
    W['rwkv_mu_b'] = _pack_rwkv_cols(row(P['rwkv_mu_b'][l]))
    W['rwkv_w0_f'] = row(P['rwkv_w0_f'][l])
    W['rwkv_w2_f'] = _pad_rows(P['rwkv_w2_f'][l], LANE).astype(BF16)
    W['rwkv_w0_b'] = row(P['rwkv_w0_b'][l])
    W['rwkv_w2_b'] = _pad_rows(P['rwkv_w2_b'][l], LANE).astype(BF16)
    W['rwkv_a0'] = row(P['rwkv_a0'][l])
    W['rwkv_a2'] = _pad_rows(P['rwkv_a2'][l], LANE).astype(BF16)
    W['rwkv_g2'] = P['rwkv_g2'][l].astype(BF16)
    for name in ('rwkv_k_k', 'rwkv_k_a', 'rwkv_r_k', 'rwkv_norm_g', 'rwkv_norm_b'):
        W[name] = row(P[name][l])
    for name in ('w_branch_gla', 'w_branch_hgrn', 'w_branch_rwkv', 'w_out', 'xattn_wq', 'xattn_wo'):
        W[name] = P[name][l].astype(BF16)
    W['peer_wq'] = P['peer_wq'][l]
    W['xattn_wkv'] = jnp.concatenate([P['xattn_wk'][l], P['xattn_wv'][l]], axis=1).astype(BF16)
    for name in ('norm_x_g', 'norm_mem_g', 'norm_ffn_g'):
        W[name] = P[name][l]
    half = PEER_DK // 2
    W['peer_sk1'] = jnp.pad(P['peer_subkeys_1'][l], ((0, 0), (0, half)))
    W['peer_sk2'] = jnp.pad(P['peer_subkeys_2'][l], ((0, 0), (half, 0)))
    W['peer_u'] = P['peer_u'][l].astype(BF16)
    W['peer_vt'] = P['peer_v'][l].astype(BF16).T
    return W


def _shared_consts():
    L = RWKV_CHUNK
    idx = jnp.arange(2 * L)
    same = (idx[:, None] // L) == (idx[None, :] // L)
    tpos = idx % L
    C = {
        'gla_ebc': _head_select(GLA_DK, GLA_HEADS),
        'gla_bdt': _head_match(MIX_W, GLA_DV, GLA_W, GLA_DK, F32),
        'hgrn_ebc': _head_select(HGRN_DK, HGRN_HEADS),
        'hgrn_bdt': _head_match(MIX_W, HGRN_DV, HGRN_W, HGRN_DK, F32),
        'hh128': _head_match(MIX_W, 128, MIX_W, 128, BF16),
        'hh64': _head_match(MIX_W, RWKV_HEAD, MIX_W, RWKV_HEAD, BF16),
        'rwkv_bd': _head_match(MIX_W, RWKV_HEAD, MIX_W, RWKV_HEAD, F32),
        'strict_f': (same & (tpos[None, :] < tpos[:, None])).astype(F32),
        'incl_f': (same & (tpos[None, :] <= tpos[:, None])).astype(F32),
        'strict_b': (same & (tpos[None, :] > tpos[:, None])).astype(F32),
        'incl_b': (same & (tpos[None, :] >= tpos[:, None])).astype(F32),
    }
    return C


def _encoder_layer(x, mem, W, C, bsz, t, n_mem):
    g_mix = W['norm_mix_g']
    z_gla = _norm_matmul(x, g_mix, W['w_gla'])
    z_hgrn = _norm_matmul(x, g_mix, W['w_hgrn'])
    z_rwkv = _norm_matmul(x, g_mix, W['w_rwkv'])
    z_gate = _norm_matmul(x, g_mix, W['w_gate'])

    gla_c = (C['gla_ebc'], C['gla_bdt'])
    o_f = _lin_attn_pass(_gla_kernel, z_gla, (W['gla_up_f'], W['gla_bias_f']) + gla_c, None, None,
                         bsz, t, GLA_W, False)
    o_gla = _lin_attn_pass(_gla_kernel, z_gla, (W['gla_up_b'], W['gla_bias_b']) + gla_c, o_f,
                           (C['hh128'], W['gla_norm_g']), bsz, t, GLA_W, True)

    hg_c = (W['hgrn_lb'], C['hgrn_ebc'], C['hgrn_bdt'])
    o_f = _lin_attn_pass(_hgrn_kernel, z_hgrn, hg_c, None, None, bsz, t, HGRN_W, False)
    o_hgrn = _lin_attn_pass(_hgrn_kernel, z_hgrn, hg_c, o_f, (C['hh128'], W['hgrn_norm_g']),
                            bsz, t, HGRN_W, True)

    prep_c = (W['rwkv_mu_f'], W['rwkv_mu_b'], W['rwkv_w0_f'], W['rwkv_w2_f'], W['rwkv_w0_b'],
              W['rwkv_w2_b'], W['rwkv_a0'], W['rwkv_a2'], W['rwkv_g2'], W['rwkv_k_k'], W['rwkv_k_a'],
              W['rwkv_r_k'], C['hh64'])
    r, kh, v, kk, b, lw_f, lw_b, g, bonus = _rwkv_prep(z_rwkv, prep_c, bsz, t)
    o_rf = _rwkv_scan(r, lw_f, kh, v, kk, b, (C['rwkv_bd'], C['strict_f'], C['incl_f']), bsz, t, False)
    o_rb = _rwkv_scan(r, lw_b, kh, v, kk, b, (C['rwkv_bd'], C['strict_b'], C['incl_b']), bsz, t, True)

    x = _merge(x, (o_gla, o_hgrn, o_rf, o_rb, bonus, g, z_gate),
               (W['w_branch_gla'], W['w_branch_hgrn'], W['w_branch_rwkv'], W['w_out'],
                W['rwkv_norm_g'], W['rwkv_norm_b'], C['hh64']))

    kv = _norm_matmul(mem, W['norm_mem_g'], W['xattn_wkv'])
    x = _xattn(x, kv, W['norm_x_g'], W['xattn_wq'], W['xattn_wo'], bsz, t, n_mem)

    q, h = _norm_matmul(x, W['norm_ffn_g'], W['peer_wq'], emit_h=True)
    s1, s2, st = _peer_topk(q, W['peer_sk1'], W['peer_sk2'])
    return _peer_experts(h, s1, s2, st, x, W['peer_u'], W['peer_vt'])


def _run_trunk(x, mem, P, weights, C):
    bsz, t, d = x.shape
    n_mem = mem.shape[1]
    x = x.reshape(bsz * t, d)
    mem = mem.reshape(bsz * n_mem, d)
    for W in weights:
        x = _encoder_layer(x, mem, W, C, bsz, t, n_mem)
    return _final_norm(x, P['final_norm_g']).reshape(bsz, t, d)


def _hgrn_lower_bounds(logits):
    sm = jax.nn.softmax(logits.astype(F32), axis=0)
    return jnp.cumsum(sm, axis=0) - sm[0]


def kernel(x_prompt, x_sample, mem_prompt, mem_sample, norm_mix_g, w_in, gla_gate_up_f, gla_gate_up_b, gla_gate_bias_f, gla_gate_bias_b, gla_norm_g, hgrn_lb_logits, hgrn_norm_g, rwkv_mu_f, rwkv_mu_b, rwkv_w0_f, rwkv_w2_f, rwkv_w0_b, rwkv_w2_b, rwkv_a0, rwkv_a2, rwkv_g2, rwkv_k_k, rwkv_k_a, rwkv_r_k, rwkv_norm_g, rwkv_norm_b, w_branch_gla, w_branch_hgrn, w_branch_rwkv, w_out, norm_x_g, norm_mem_g, xattn_wq, xattn_wk, xattn_wv, xattn_wo, norm_ffn_g, peer_wq, peer_subkeys_1, peer_subkeys_2, peer_u, peer_v, final_norm_g):
    P = dict(norm_mix_g=norm_mix_g, w_in=w_in, gla_gate_up_f=gla_gate_up_f, gla_gate_up_b=gla_gate_up_b,
             gla_gate_bias_f=gla_gate_bias_f, gla_gate_bias_b=gla_gate_bias_b, gla_norm_g=gla_norm_g,
             hgrn_lb_logits=hgrn_lb_logits, hgrn_norm_g=hgrn_norm_g, rwkv_mu_f=rwkv_mu_f, rwkv_mu_b=rwkv_mu_b,
             rwkv_w0_f=rwkv_w0_f, rwkv_w2_f=rwkv_w2_f, rwkv_w0_b=rwkv_w0_b, rwkv_w2_b=rwkv_w2_b,
             rwkv_a0=rwkv_a0, rwkv_a2=rwkv_a2, rwkv_g2=rwkv_g2, rwkv_k_k=rwkv_k_k, rwkv_k_a=rwkv_k_a,
             rwkv_r_k=rwkv_r_k, rwkv_norm_g=rwkv_norm_g, rwkv_norm_b=rwkv_norm_b, w_branch_gla=w_branch_gla,
             w_branch_hgrn=w_branch_hgrn, w_branch_rwkv=w_branch_rwkv, w_out=w_out, norm_x_g=norm_x_g,
             norm_mem_g=norm_mem_g, xattn_wq=xattn_wq, xattn_wk=xattn_wk, xattn_wv=xattn_wv, xattn_wo=xattn_wo,
             norm_ffn_g=norm_ffn_g, peer_wq=peer_wq, peer_subkeys_1=peer_subkeys_1,
             peer_subkeys_2=peer_subkeys_2, peer_u=peer_u, peer_v=peer_v, final_norm_g=final_norm_g)
    depth = w_in.shape[0]
    lbs = _hgrn_lower_bounds(hgrn_lb_logits)
    weights = [_layer_weights(P, l, lbs[l]) for l in range(depth)]
    C = _shared_consts()
    return (_run_trunk(x_prompt, mem_prompt, P, weights, C),
            _run_trunk(x_sample, mem_sample, P, weights, C))
```

```python
import functools

import jax
import jax.numpy as jnp
from jax import lax
from jax.experimental import pallas as pl
from jax.experimental.pallas import tpu as pltpu

F32 = jnp.float32
BF16 = jnp.bfloat16

D_MODEL = 1024
EPS = 1e-6
LOG_FLOOR = 1e-30
MIX_W = 512
GLA_HEADS, GLA_DK, GLA_DV = 4, 64, 128
GLA_GATE_RANK = 16
GLA_GATE_NORM = 16.0
HGRN_HEADS, HGRN_DK, HGRN_DV = 4, 128, 128
RWKV_HEADS, RWKV_HEAD = 8, 64
RWKV_DECAY_RANK, RWKV_AAA_RANK, RWKV_GATE_RANK = 64, 64, 128
RWKV_DECAY_SCALE = 0.606531
RWKV_GN_EPS = 64e-5
X_HEADS = 4
X_HEAD = D_MODEL // X_HEADS
PEER_HEADS, PEER_DK, PEER_NKEYS, PEER_TOPK = 8, 128, 128, 16

GLA_W = GLA_HEADS * GLA_DK
GLA_IN = 2 * GLA_W + 2 * MIX_W + 2 * GLA_GATE_RANK
HGRN_W = HGRN_HEADS * HGRN_DK
HGRN_IN = 5 * MIX_W
RWKV_IN = 3 * MIX_W + 2 * RWKV_DECAY_RANK + RWKV_AAA_RANK + RWKV_GATE_RANK
GATE_OFF = GLA_IN + HGRN_IN + RWKV_IN

LANE = 128
SUBLANES = 8
ZGLA_W = 2 * GLA_W + 2 * MIX_W + LANE
ZRWKV_W = 3 * MIX_W + 4 * LANE

ROW_TILE = 512
MERGE_TILE = 256
GLA_CHUNK = 32
GLA_TB = 256
RWKV_CHUNK = 64
RWKV_TB = 256
PREP_TB = 256
XATTN_TQ = 256
TOPK_TB = 128
PEER_TB = 256
PEER_ET = 2048
PEER_MM = 1024
VMEM_LIMIT = 48 * 1024 * 1024


def _params(sem):
    return pltpu.CompilerParams(dimension_semantics=sem, vmem_limit_bytes=VMEM_LIMIT)


def _dot(a, b):
    return jnp.dot(a.astype(BF16), b.astype(BF16), preferred_element_type=F32)


def _dot_nt(a, b):
    return lax.dot_general(a.astype(BF16), b.astype(BF16), (((1,), (1,)), ((), ())),
                           preferred_element_type=F32)


def _dot_nt_f32(a, b):
    return lax.dot_general(a, b, (((1,), (1,)), ((), ())), precision=lax.Precision.HIGHEST,
                           preferred_element_type=F32)


def _dot_tn(a, b):
    return lax.dot_general(a.astype(BF16), b.astype(BF16), (((0,), (0,)), ((), ())),
                           preferred_element_type=F32)


def _split3(x):
    hi = x.astype(BF16)
    r1 = x - hi.astype(F32)
    mid = r1.astype(BF16)
    lo = (r1 - mid.astype(F32)).astype(BF16)
    return hi, mid, lo


def _dot_exact_rhs(x, m):
    hi, mid, lo = _split3(x)
    f = lambda p: jnp.dot(p, m, preferred_element_type=F32)
    return f(hi) + f(mid) + f(lo)


def _dot_exact_lhs(m, x):
    hi, mid, lo = _split3(x)
    f = lambda p: jnp.dot(m, p, preferred_element_type=F32)
    return f(hi) + f(mid) + f(lo)


def _tri_incl(n):
    r = lax.broadcasted_iota(jnp.int32, (n, n), 0)
    c = lax.broadcasted_iota(jnp.int32, (n, n), 1)
    return (r >= c).astype(BF16)


def _rmsnorm(x, g):
    return x * lax.rsqrt(jnp.mean(x * x, axis=-1, keepdims=True) + EPS) * g


def _log_sigmoid(x):
    return jnp.minimum(x, 0.0) - jnp.log(1.0 + jnp.exp(-jnp.abs(x)))


def _silu(x):
    return x * jax.nn.sigmoid(x)


def _norm_matmul_kernel(x_ref, g_ref, w_ref, o_ref, *h_ref):
    h = _rmsnorm(x_ref[...], g_ref[...])
    if w_ref.dtype == F32:
        o_ref[...] = jnp.dot(h, w_ref[...], precision=lax.Precision.HIGHEST, preferred_element_type=F32)
    else:
        o_ref[...] = jnp.dot(h.astype(BF16), w_ref[...], preferred_element_type=F32)
    if h_ref:
        h_ref[0][...] = h.astype(h_ref[0].dtype)


def _norm_matmul(x, g, w, emit_h=False):
    n, d = x.shape
    c = w.shape[1]
    tm = min(ROW_TILE, n)
    out_shape = [jax.ShapeDtypeStruct((n, c), F32)]
    out_specs = [pl.BlockSpec((tm, c), lambda i: (i, 0))]
    if emit_h:
        out_shape.append(jax.ShapeDtypeStruct((n, d), BF16))
        out_specs.append(pl.BlockSpec((tm, d), lambda i: (i, 0)))
    res = pl.pallas_call(
        _norm_matmul_kernel,
        grid=(n // tm,),
        in_specs=[pl.BlockSpec((tm, d), lambda i: (i, 0)),
                  pl.BlockSpec((1, d), lambda i: (0, 0)),
                  pl.BlockSpec((d, c), lambda i: (0, 0))],
        out_specs=out_specs,
        out_shape=out_shape,
        compiler_params=_params(("parallel",)),
    )(x, g.reshape(1, d), w)
    return res if emit_h else res[0]


def _final_norm_kernel(x_ref, g_ref, o_ref):
    o_ref[...] = _rmsnorm(x_ref[...], g_ref[...])


def _final_norm(x, g):
    n, d = x.shape
    tm = min(ROW_TILE, n)
    return pl.pallas_call(
        _final_norm_kernel,
        grid=(n // tm,),
        in_specs=[pl.BlockSpec((tm, d), lambda i: (i, 0)), pl.BlockSpec((1, d), lambda i: (0, 0))],
        out_specs=pl.BlockSpec((tm, d), lambda i: (i, 0)),
        out_shape=jax.ShapeDtypeStruct((n, d), F32),
        compiler_params=_params(("parallel",)),
    )(x, g.reshape(1, d))


def _gla_time_block(q, k, v, g, o_scr, st_scr, qs, ks, vs, cs, cp, ebc_ref, bdt_ref, tri_ref, blk_ref,
                    reverse):
    tb, w = q.shape
    hv = v.shape[1]
    C = GLA_CHUNK
    nchunks = tb // C
    cum = _dot_exact_lhs(tri_ref[...], g)
    tot = _dot_exact_lhs(blk_ref[...], g)
    if reverse:
        cum = tot - cum + g
    qs[...] = q
    cs[...] = cum
    pads = [jnp.zeros((C, w), F32), jnp.zeros((C, w), F32), jnp.zeros((C, hv), F32)]
    for dst, val, pad in zip((ks, cp, vs), (k, cum, v), pads):
        ext = jnp.concatenate([val, pad] if reverse else [pad, val], axis=0)
        for b in range(SUBLANES):
            shift = ((tb + C - b) % (tb + C)) if reverse else b
            dst[b] = pltpu.roll(ext, shift, 0) if shift else ext
    pos = lax.broadcasted_iota(jnp.int32, (tb, 1), 0) % C
    dk = w // (hv // LANE)
    for h in range(hv // LANE):
        lt = (h * dk) // LANE
        ls = slice(lt * LANE, (lt + 1) * LANE)
        hs = slice(h * LANE, (h + 1) * LANE)

        def body(a, o, h=h, ls=ls, hs=hs):
            a8 = a * SUBLANES
            off = pl.multiple_of(a8 if reverse else C - a8, SUBLANES)
            for b in range(SUBLANES):
                d = a8 + b
                kr = ks[b, pl.ds(off, tb), ls]
                cr = cp[b, pl.ds(off, tb), ls]
                vr = vs[b, pl.ds(off, tb), hs]
                live = (pos <= C - 1 - d) if reverse else (pos >= d)
                w_ij = qs[:, ls] * kr * jnp.exp(jnp.minimum(cs[:, ls] - cr, 0.0))
                w_ij = jnp.where(live, w_ij, 0.0).astype(BF16)
                o = o + jnp.dot(w_ij, ebc_ref[h], preferred_element_type=F32) * vr
            return o

        o_scr[:, hs] = lax.fori_loop(0, C // SUBLANES, body, jnp.zeros((tb, LANE), F32))
    qe = q * jnp.exp(cum)
    kd = k * jnp.exp(tot - cum)
    dec = jnp.exp(tot)
    order = range(nchunks - 1, -1, -1) if reverse else range(nchunks)
    for c in order:
        sl = slice(c * GLA_CHUNK, (c + 1) * GLA_CHUNK)
        st = st_scr[...]
        o_scr[sl, :] = o_scr[sl, :] + _dot_nt(qe[sl], st)
        st_scr[...] = (st * dec[c * GLA_CHUNK:c * GLA_CHUNK + 1]
                       + _dot_tn(v[sl], kd[sl]) * bdt_ref[...])


def _head_rms_gate(o, og, hh_ref, ng_ref, dv):
    ms = _dot_exact_rhs(o * o, hh_ref[...]) * (1.0 / dv)
    return o * lax.rsqrt(ms + EPS) * ng_ref[...] * _silu(og)


def _gla_kernel(*refs, reverse, final):
    if final:
        (z_ref, up_ref, bias_ref, ebc_ref, bdt_ref, tri_ref, blk_ref, oprev_ref, hh_ref, ng_ref,
         o_ref, st_scr, o_scr, qs, ks, vs, cs, cp) = refs
    else:
        (z_ref, up_ref, bias_ref, ebc_ref, bdt_ref, tri_ref, blk_ref,
         o_ref, st_scr, o_scr, qs, ks, vs, cs, cp) = refs

    @pl.when(pl.program_id(1) == 0)
    def _():
        st_scr[...] = jnp.zeros_like(st_scr)

    z = z_ref[...]
    q = z[:, 0:GLA_W] * (GLA_DK ** -0.5)
    k = z[:, GLA_W:2 * GLA_W]
    v = z[:, 2 * GLA_W:2 * GLA_W + MIX_W]
    gd = z[:, 2 * GLA_W + 2 * MIX_W:ZGLA_W]
    g = _log_sigmoid(_dot(gd, up_ref[...]) + bias_ref[...]) * (1.0 / GLA_GATE_NORM)
    _gla_time_block(q, k, v, g, o_scr, st_scr, qs, ks, vs, cs, cp, ebc_ref, bdt_ref, tri_ref, blk_ref, reverse)
    if final:
        og = z[:, 2 * GLA_W + MIX_W:2 * GLA_W + 2 * MIX_W]
        o_ref[...] = _head_rms_gate(oprev_ref[...] + o_scr[...], og, hh_ref, ng_ref, GLA_DV)
    else:
        o_ref[...] = o_scr[...]


def _hgrn_kernel(*refs, reverse, final):
    if final:
        (z_ref, lb_ref, ebc_ref, bdt_ref, tri_ref, blk_ref, oprev_ref, hh_ref, ng_ref,
         o_ref, st_scr, o_scr, qs, ks, vs, cs, cp) = refs
    else:
        z_ref, lb_ref, ebc_ref, bdt_ref, tri_ref, blk_ref, o_ref, st_scr, o_scr, qs, ks, vs, cs, cp = refs

    @pl.when(pl.program_id(1) == 0)
    def _():
        st_scr[...] = jnp.zeros_like(st_scr)

    z = z_ref[...]
    lb = lb_ref[...]
    q = _silu(z[:, 0:HGRN_W])
    zf = z[:, (2 if reverse else 1) * HGRN_W:(3 if reverse else 2) * HGRN_W]
    v = z[:, 3 * HGRN_W:4 * HGRN_W]
    f = lb + (1.0 - lb) * jax.nn.sigmoid(zf)
    g = jnp.log(jnp.maximum(f, LOG_FLOOR))
    k = (1.0 - lb) * jax.nn.sigmoid(-zf)
    _gla_time_block(q, k, v, g, o_scr, st_scr, qs, ks, vs, cs, cp, ebc_ref, bdt_ref, tri_ref, blk_ref, reverse)
    if final:
        og = z[:, 4 * HGRN_W:5 * HGRN_W]
        o_ref[...] = _head_rms_gate(oprev_ref[...] + o_scr[...], og, hh_ref, ng_ref, HGRN_DV)
    else:
        o_ref[...] = o_scr[...]


def _seq_row_map(nblk, reverse):
    if reverse:
        return lambda b, i: (b * nblk + nblk - 1 - i, 0)
    return lambda b, i: (b * nblk + i, 0)


def _const_map(b, i):
    return (0, 0)


def _lin_attn_pass(kernel, z, consts, o_prev, final_consts, bsz, t, w, reverse):
    n, zc = z.shape
    tb = min(GLA_TB, t)
    nblk = t // tb
    rmap = _seq_row_map(nblk, reverse)
    final = o_prev is not None
    r = jnp.arange(tb)
    same = (r[:, None] // GLA_CHUNK) == (r[None, :] // GLA_CHUNK)
    consts = tuple(consts) + ((same & (r[None, :] <= r[:, None])).astype(BF16), same.astype(BF16))
    args = [z] + list(consts)
    in_specs = [pl.BlockSpec((tb, zc), rmap)]
    in_specs += [pl.BlockSpec(c.shape, lambda b, i, nd=c.ndim: (0,) * nd) for c in consts]
    if final:
        args += [o_prev] + list(final_consts)
        in_specs += [pl.BlockSpec((tb, MIX_W), rmap)]
        in_specs += [pl.BlockSpec(c.shape, _const_map) for c in final_consts]
    return pl.pallas_call(
        functools.partial(kernel, reverse=reverse, final=final),
        grid=(bsz, nblk),
        in_specs=in_specs,
        out_specs=pl.BlockSpec((tb, MIX_W), rmap),
        out_shape=jax.ShapeDtypeStruct((n, MIX_W), F32),
        scratch_shapes=[pltpu.VMEM((MIX_W, w), F32), pltpu.VMEM((tb, MIX_W), F32),
                        pltpu.VMEM((tb, w), F32), pltpu.VMEM((SUBLANES, tb + GLA_CHUNK, w), F32),
                        pltpu.VMEM((SUBLANES, tb + GLA_CHUNK, MIX_W), F32), pltpu.VMEM((tb, w), F32),
                        pltpu.VMEM((SUBLANES, tb + GLA_CHUNK, w), F32)],
        compiler_params=_params(("parallel", "arbitrary")),
    )(*args)


def _head_select(dk, heads):
    h = jnp.arange(heads)[:, None]
    lane = ((h * dk) // LANE) * LANE + jnp.arange(LANE)[None, :]
    own = (lane // dk == h).astype(BF16)
    return jnp.broadcast_to(own[:, :, None], (heads, LANE, LANE))


def _head_match(n_rows, row_blk, n_cols, col_blk, dtype):
    r = jnp.arange(n_rows) // row_blk
    c = jnp.arange(n_cols) // col_blk
    return (r[:, None] == c[None, :]).astype(dtype)


def _rwkv_prep_kernel(z_ref, zp_ref, zn_ref, muf_ref, mub_ref, w0f_ref, w2f_ref, w0b_ref, w2b_ref,
                      a0_ref, a2_ref, g2_ref, kk_ref, ka_ref, rk_ref, hh_ref,
                      r_out, kh_out, v_out, kkn_out, b_out, lwf_out, lwb_out, g_out, bonus_out):
    i = pl.program_id(1)
    last = pl.num_programs(1) - 1
    z = z_ref[...]
    tb = z.shape[0]
    rows = lax.broadcasted_iota(jnp.int32, (tb, 1), 0)
    hp = jnp.where(i == 0, 0.0, zp_ref[7:8, :])
    hn = jnp.where(i == last, 0.0, zn_ref[0:1, :])
    prev = jnp.where(rows == 0, hp, pltpu.roll(z, 1, 0))
    nxt = jnp.where(rows == tb - 1, hn, pltpu.roll(z, tb - 1, 0))
    p = z + muf_ref[...] * (prev - z) + mub_ref[...] * (nxt - z)
    r = p[:, 0:MIX_W]
    k = p[:, MIX_W:2 * MIX_W]
    v = p[:, 2 * MIX_W:3 * MIX_W]
    o = 3 * MIX_W
    wdf, wdb, ad, gd = (p[:, o + j * LANE:o + (j + 1) * LANE] for j in range(4))
    lwf = -RWKV_DECAY_SCALE * jax.nn.sigmoid(w0f_ref[...] + _dot(jnp.tanh(wdf), w2f_ref[...]))
    lwb = -RWKV_DECAY_SCALE * jax.nn.sigmoid(w0b_ref[...] + _dot(jnp.tanh(wdb), w2b_ref[...]))
    a = jax.nn.sigmoid(a0_ref[...] + _dot(ad, a2_ref[...]))
    g = _dot(jax.nn.sigmoid(gd), g2_ref[...])
    kk = k * kk_ref[...]
    ss = _dot_exact_rhs(kk * kk, hh_ref[...])
    kk = kk / jnp.maximum(jnp.sqrt(ss), 1e-12)
    kh = k * (1.0 + (a - 1.0) * ka_ref[...])
    bonus = _dot_exact_rhs(r * kh * rk_ref[...], hh_ref[...]) * v
    r_out[...] = r
    kh_out[...] = kh
    v_out[...] = v
    kkn_out[...] = kk
    b_out[...] = kk * a
    lwf_out[...] = lwf
    lwb_out[...] = lwb
    g_out[...] = g
    bonus_out[...] = bonus


def _rwkv_prep(z, consts, bsz, t):
    n, zc = z.shape
    tb = min(PREP_TB, t)
    nblk = t // tb
    hb = tb // 8
    nrow8 = n // 8
    rmap = _seq_row_map(nblk, False)
    pmap = lambda b, i: (jnp.maximum((b * nblk + i) * hb - 1, 0), 0)
    nmap = lambda b, i: (jnp.minimum((b * nblk + i + 1) * hb, nrow8 - 1), 0)
    in_specs = [pl.BlockSpec((tb, zc), rmap), pl.BlockSpec((8, zc), pmap), pl.BlockSpec((8, zc), nmap)]
    in_specs += [pl.BlockSpec(c.shape, _const_map) for c in consts]
    return pl.pallas_call(
        _rwkv_prep_kernel,
        grid=(bsz, nblk),
        in_specs=in_specs,
        out_specs=[pl.BlockSpec((tb, MIX_W), rmap)] * 9,
        out_shape=[jax.ShapeDtypeStruct((n, MIX_W), F32)] * 9,
        compiler_params=_params(("parallel", "parallel")),
    )(z, z, z, *consts)


def _rwkv_scan_kernel(r_ref, lw_ref, kh_ref, v_ref, kk_ref, b_ref, bd_ref, mstrict_ref, mincl_ref,
                      o_ref, s_scr, *, reverse):
    @pl.when(pl.program_id(1) == 0)
    def _():
        s_scr[...] = jnp.zeros_like(s_scr)

    L = RWKV_CHUNK
    tb = r_ref.shape[0]
    nchunks = tb // L
    tri = _tri_incl(L)
    lane = lax.broadcasted_iota(jnp.int32, (1, LANE), 1)
    m0 = (lane < RWKV_HEAD).astype(F32)
    m1 = 1.0 - m0
    eye = (lax.broadcasted_iota(jnp.int32, (2 * L, 2 * L), 0)
           == lax.broadcasted_iota(jnp.int32, (2 * L, 2 * L), 1)).astype(F32)
    mstrict = mstrict_ref[...]
    mincl = mincl_ref[...]
    groups = range(MIX_W // LANE)
    lanes = lambda p: slice(p * LANE, (p + 1) * LANE)
    stack2 = lambda x: jnp.concatenate([x * m0, x * m1], axis=0)

    pre = []
    for c in range(nchunks):
        sl = slice(c * L, (c + 1) * L)
        r, lw, kh, v, kk, b = (ref[sl, :] for ref in (r_ref, lw_ref, kh_ref, v_ref, kk_ref, b_ref))
        cum = _dot_exact_lhs(tri, lw)
        tot = cum[L - 1:L]
        inc = (tot - cum + lw) if reverse else cum
        exc = inc - lw
        mid = inc[L // 2:L // 2 + 1]
        e_pos = jnp.exp(inc - mid)
        e_neg = jnp.exp(mid - inc)
        e_end = jnp.exp(tot - inc)
        pre.append(dict(sl=sl, v=v, rt=r * e_pos, at=-kk * jnp.exp(exc - mid), bt=b * e_neg, kt=kh * e_neg,
                        a_abs=-kk * jnp.exp(exc), r_abs=r * jnp.exp(inc), dec=jnp.exp(tot),
                        wr=jnp.concatenate([b * e_end, kh * e_end], axis=0)))
    cells = [(c, p) for c in range(nchunks) for p in groups]
    gram = {}
    for c, p in cells:
        d = pre[c]
        lhs = jnp.concatenate([stack2(d['at'][:, lanes(p)]), stack2(d['rt'][:, lanes(p)])], axis=0)
        b_p, k_p = d['bt'][:, lanes(p)], d['kt'][:, lanes(p)]
        gram[c, p] = _dot_nt(lhs, jnp.concatenate([b_p, b_p, k_p, k_p], axis=0))
    n_ab = {k: g[0:2 * L, 0:2 * L] * mstrict for k, g in gram.items()}
    a_ak = {k: g[0:2 * L, 2 * L:4 * L] * mstrict for k, g in gram.items()}
    a_rb = {k: g[2 * L:4 * L, 0:2 * L] * mincl for k, g in gram.items()}
    a_rk = {k: g[2 * L:4 * L, 2 * L:4 * L] * mincl for k, g in gram.items()}
    v_bd = {(c, p): stack2(pre[c]['v'][:, lanes(p)]) for c, p in cells}
    x0 = {k: _dot(a_ak[k], v_bd[k]) for k in cells}
    o0 = {k: _dot(a_rk[k], v_bd[k]) for k in cells}
    tinv = {k: eye + n_ab[k] for k in cells}
    pw = dict(n_ab)
    for _ in range(5):
        pw = {k: _dot(pw[k], pw[k]) for k in cells}
        tinv = {k: tinv[k] + _dot(pw[k], tinv[k]) for k in cells}

    fold2 = lambda x: x[0:L] + x[L:2 * L]
    for c in (range(nchunks - 1, -1, -1) if reverse else range(nchunks)):
        d = pre[c]
        s = s_scr[...]
        am = _dot_nt(d['a_abs'], s)
        rm = _dot_nt(d['r_abs'], s)
        us = [_dot(tinv[c, p], stack2(am[:, lanes(p)]) + x0[c, p]) for p in groups]
        o_bd = [_dot(a_rb[c, p], us[p]) + o0[c, p] for p in groups]
        o_ref[d['sl'], :] = rm + jnp.concatenate([fold2(o) for o in o_bd], axis=1)
        u_all = jnp.concatenate([fold2(u) for u in us], axis=1)
        upd = _dot_tn(jnp.concatenate([u_all, d['v']], axis=0), d['wr'])
        s_scr[...] = s * d['dec'] + upd * bd_ref[...]


def _rwkv_scan(r, lw, kh, v, kk, b, consts, bsz, t, reverse):
    n = r.shape[0]
    tb = min(RWKV_TB, t)
    nblk = t // tb
    rmap = _seq_row_map(nblk, reverse)
    row_spec = pl.BlockSpec((tb, MIX_W), rmap)
    return pl.pallas_call(
        functools.partial(_rwkv_scan_kernel, reverse=reverse),
        grid=(bsz, nblk),
        in_specs=[row_spec] * 6 + [pl.BlockSpec(c.shape, _const_map) for c in consts],
        out_specs=row_spec,
        out_shape=jax.ShapeDtypeStruct((n, MIX_W), F32),
        scratch_shapes=[pltpu.VMEM((MIX_W, MIX_W), F32)],
        compiler_params=_params(("parallel", "arbitrary")),
    )(r, lw, kh, v, kk, b, *consts)


def _merge_kernel(x_ref, ogla_ref, ohg_ref, orf_ref, orb_ref, bonus_ref, g_ref, zg_ref,
                  wg_ref, wh_ref, wr_ref, wo_ref, ng_ref, nb_ref, hh_ref, o_ref):
    o = orf_ref[...] + orb_ref[...]
    inv = 1.0 / RWKV_HEAD
    mean = _dot_exact_rhs(o, hh_ref[...]) * inv
    d = o - mean
    var = _dot_exact_rhs(d * d, hh_ref[...]) * inv
    on = d * lax.rsqrt(var + RWKV_GN_EPS) * ng_ref[...] + nb_ref[...]
    orw = (on + bonus_ref[...]) * g_ref[...]
    zg = zg_ref[...]
    gate = lambda j: jax.nn.sigmoid(zg[:, j * D_MODEL:(j + 1) * D_MODEL])
    merged = (gate(0) * _dot(ogla_ref[...], wg_ref[...])
              + gate(1) * _dot(ohg_ref[...], wh_ref[...])
              + gate(2) * _dot(orw, wr_ref[...]))
    o_ref[...] = x_ref[...] + _dot(merged, wo_ref[...])


def _merge(x, rows, consts):
    n, d = x.shape
    tm = min(MERGE_TILE, n)
    rspec = lambda a: pl.BlockSpec((tm, a.shape[1]), lambda i: (i, 0))
    return pl.pallas_call(
        _merge_kernel,
        grid=(n // tm,),
        in_specs=[rspec(x)] + [rspec(a) for a in rows]
        + [pl.BlockSpec(c.shape, lambda i: (0, 0)) for c in consts],
        out_specs=pl.BlockSpec((tm, d), lambda i: (i, 0)),
        out_shape=jax.ShapeDtypeStruct((n, d), F32),
        compiler_params=_params(("parallel",)),
    )(x, *rows, *consts)


def _xattn_kernel(x_ref, kv_ref, g_ref, wq_ref, wo_ref, o_ref):
    x = x_ref[...]
    h = _rmsnorm(x, g_ref[...])
    q = _dot(h, wq_ref[...])
    kv = kv_ref[...]
    outs = []
    for hd in range(X_HEADS):
        ls = slice(hd * X_HEAD, (hd + 1) * X_HEAD)
        s = _dot_nt(q[:, ls], kv[:, ls]) * (X_HEAD ** -0.5)
        s = s - jnp.max(s, axis=-1, keepdims=True)
        e = jnp.exp(s)
        pr = e / jnp.sum(e, axis=-1, keepdims=True)
        outs.append(_dot(pr, kv[:, D_MODEL + hd * X_HEAD:D_MODEL + (hd + 1) * X_HEAD]))
    o_ref[...] = x + _dot(jnp.concatenate(outs, axis=1), wo_ref[...])


def _xattn(x, kv, g, wq, wo, bsz, t, n_mem):
    n, d = x.shape
    tq = min(XATTN_TQ, t)
    nblk = t // tq
    return pl.pallas_call(
        _xattn_kernel,
        grid=(bsz, nblk),
        in_specs=[pl.BlockSpec((tq, d), lambda b, i: (b * nblk + i, 0)),
                  pl.BlockSpec((n_mem, 2 * d), lambda b, i: (b, 0)),
                  pl.BlockSpec((1, d), _const_map),
                  pl.BlockSpec((d, d), _const_map),
                  pl.BlockSpec((d, d), _const_map)],
        out_specs=pl.BlockSpec((tq, d), lambda b, i: (b * nblk + i, 0)),
        out_shape=jax.ShapeDtypeStruct((n, d), F32),
        compiler_params=_params(("parallel", "parallel")),
    )(x, kv, g.reshape(1, d), wq, wo)


def _top16_rows(x):
    nrow = x.shape[0]
    rows = lax.broadcasted_iota(jnp.int32, x.shape, 0)
    vals = []
    for _ in range(PEER_TOPK):
        m = jnp.max(x, axis=0, keepdims=True)
        pos = jnp.min(jnp.where(x == m, rows, nrow), axis=0, keepdims=True)
        vals.append(m)
        x = jnp.where(rows == pos, -jnp.inf, x)
    return jnp.concatenate(vals, 0)


def _peer_topk_kernel(q_ref, sk1_ref, sk2_ref, s1_ref, s2_ref, st_ref):
    q = q_ref[...]
    thr, mx1, mx2, rz = [], [], [], []
    for hd in range(PEER_HEADS):
        hs = slice(hd * PEER_NKEYS, (hd + 1) * PEER_NKEYS)
        qh = q[:, hd * PEER_DK:(hd + 1) * PEER_DK]
        s1 = _dot_nt_f32(sk1_ref[...], qh)
        s2 = _dot_nt_f32(sk2_ref[...], qh)
        s1_ref[hs, :] = s1
        s2_ref[hs, :] = s2
        v1 = _top16_rows(s1)
        v2 = _top16_rows(s2)
        half = PEER_TOPK // 2
        cand = jnp.concatenate([v1[0:1] + v2] + [v1[a:a + 1] + v2[0:half] for a in range(1, half)]
                               + [v1[half:] + v2[0:1]], axis=0)
        top = _top16_rows(cand)
        thr.append(top[PEER_TOPK - 1:PEER_TOPK])
        mx1.append(v1[0:1])
        mx2.append(v2[0:1])
        rz.append(1.0 / jnp.sum(jnp.exp(top - top[0:1]), axis=0, keepdims=True))
    st_ref[...] = jnp.concatenate(thr + mx1 + mx2 + rz, axis=0)


def _peer_topk(q, sk1p, sk2p):
    n, d = q.shape
    tb = min(TOPK_TB, n)
    nrow = PEER_HEADS * PEER_NKEYS
    col = lambda r: pl.BlockSpec((r, tb), lambda i: (0, i))
    return pl.pallas_call(
        _peer_topk_kernel,
        grid=(n // tb,),
        in_specs=[pl.BlockSpec((tb, d), lambda i: (i, 0)),
                  pl.BlockSpec(sk1p.shape, lambda i: (0, 0)),
                  pl.BlockSpec(sk2p.shape, lambda i: (0, 0))],
        out_specs=[col(nrow), col(nrow), col(4 * PEER_HEADS)],
        out_shape=[jax.ShapeDtypeStruct((nrow, n), F32), jax.ShapeDtypeStruct((nrow, n), F32),
                   jax.ShapeDtypeStruct((4 * PEER_HEADS, n), F32)],
        compiler_params=_params(("parallel",)),
    )(q, sk1p, sk2p)


def _peer_expert_kernel(h_ref, u_ref, vt_ref, s1_ref, s2_ref, st_ref, x_ref, o_ref,
                        acc_scr, e1_scr, e2_scr):
    j = pl.program_id(1)
    nh, nk = PEER_HEADS, PEER_NKEYS

    @pl.when(j == 0)
    def _():
        acc_scr[...] = jnp.zeros_like(acc_scr)
        for hd in range(nh):
            hs = slice(hd * nk, (hd + 1) * nk)
            e1_scr[hs, :] = jnp.exp(s1_ref[hs, :] - st_ref[nh + hd:nh + hd + 1, :])
            e2_scr[hs, :] = (jnp.exp(s2_ref[hs, :] - st_ref[2 * nh + hd:2 * nh + hd + 1, :])
                             * st_ref[3 * nh + hd:3 * nh + hd + 1, :])

    et, tb = u_ref.shape[0], h_ref.shape[0]
    sub = PEER_MM // nk
    for sl in range(et // PEER_MM):
        rows = slice(sl * PEER_MM, (sl + 1) * PEER_MM)
        act = _dot_nt(u_ref[rows, :], h_ref[...])
        coef = []
        for il in range(sub):
            i1 = (j * (et // PEER_MM) + sl) * sub + il
            parts = []
            s1_rows = [s1_ref[pl.ds(hd * nk + i1, 1), :] for hd in range(nh)]
            e1_rows = [e1_scr[pl.ds(hd * nk + i1, 1), :] for hd in range(nh)]
            for lt in range(tb // LANE):
                ts = slice(lt * LANE, (lt + 1) * LANE)
                gate = jnp.zeros((nk, LANE), F32)
                for hd in range(nh):
                    hs = slice(hd * nk, (hd + 1) * nk)
                    score = s1_rows[hd][:, ts] + s2_ref[hs, ts]
                    weight = e1_rows[hd][:, ts] * e2_scr[hs, ts]
                    gate = gate + jnp.where(score >= st_ref[hd:hd + 1, ts], weight, 0.0)
                a = act[il * nk:(il + 1) * nk, ts]
                gelu = 0.5 * a * (1.0 + lax.erf(a * (2.0 ** -0.5)))
                parts.append((gate * gelu).astype(BF16))
            coef.append(jnp.concatenate(parts, axis=1))
        acc_scr[...] += jnp.dot(vt_ref[:, rows], jnp.concatenate(coef, axis=0),
                                preferred_element_type=F32)

    @pl.when(j == pl.num_programs(1) - 1)
    def _():
        o_ref[...] = x_ref[...] + acc_scr[...].T


def _peer_experts(h, s1, s2, st, x, u, vt):
    n, d = x.shape
    ne = u.shape[0]
    tb = min(PEER_TB, n)
    et = PEER_ET
    nrow = s1.shape[0]
    col = lambda r: pl.BlockSpec((r, tb), lambda i, j: (0, i))
    row = pl.BlockSpec((tb, d), lambda i, j: (i, 0))
    return pl.pallas_call(
        _peer_expert_kernel,
        grid=(n // tb, ne // et),
        in_specs=[row, pl.BlockSpec((et, d), lambda i, j: (j, 0)),
                  pl.BlockSpec((d, et), lambda i, j: (0, j)),
                  col(nrow), col(nrow), col(st.shape[0]), row],
        out_specs=row,
        out_shape=jax.ShapeDtypeStruct((n, d), F32),
        scratch_shapes=[pltpu.VMEM((d, tb), F32), pltpu.VMEM((nrow, tb), F32),
                        pltpu.VMEM((nrow, tb), F32)],
        compiler_params=_params(("parallel", "arbitrary")),
    )(h, u, vt, s1, s2, st, x)


def _pad_cols(a, width):
    return jnp.pad(a, ((0, 0), (0, width - a.shape[1])))


def _pad_rows(a, height):
    return jnp.pad(a, ((0, height - a.shape[0]), (0, 0)))


def _pack_rwkv_cols(a):
    o = 3 * MIX_W
    wf = a[:, o:o + RWKV_DECAY_RANK]
    wb = a[:, o + RWKV_DECAY_RANK:o + 2 * RWKV_DECAY_RANK]
    ad = a[:, o + 2 * RWKV_DECAY_RANK:o + 2 * RWKV_DECAY_RANK + RWKV_AAA_RANK]
    gd = a[:, o + 2 * RWKV_DECAY_RANK + RWKV_AAA_RANK:]
    return jnp.concatenate([a[:, :o], _pad_cols(wf, LANE), _pad_cols(wb, LANE), _pad_cols(ad, LANE),
                            _pad_cols(gd, LANE)], axis=1)


def _layer_weights(P, l, lb):
    w_in = P['w_in'][l]
    row = lambda a: a.reshape(1, -1).astype(F32)
    W = {}
    W['norm_mix_g'] = P['norm_mix_g'][l]
    W['w_gla'] = _pad_cols(w_in[:, :GLA_IN], ZGLA_W).astype(BF16)
    W['w_hgrn'] = w_in[:, GLA_IN:GLA_IN + HGRN_IN].astype(BF16)
    W['w_rwkv'] = _pack_rwkv_cols(w_in[:, GLA_IN + HGRN_IN:GATE_OFF]).astype(BF16)
    W['w_gate'] = w_in[:, GATE_OFF:].astype(BF16)
    W['gla_up_f'] = _pad_rows(P['gla_gate_up_f'][l], LANE).astype(BF16)
    W['gla_up_b'] = _pad_rows(jnp.concatenate(
        [jnp.zeros_like(P['gla_gate_up_b'][l]), P['gla_gate_up_b'][l]], axis=0), LANE).astype(BF16)
    W['gla_bias_f'] = row(P['gla_gate_bias_f'][l])
    W['gla_bias_b'] = row(P['gla_gate_bias_b'][l])
    W['gla_norm_g'] = row(P['gla_norm_g'][l])
    W['hgrn_lb'] = row(lb)
    W['hgrn_norm_g'] = row(P['hgrn_norm_g'][l])
    W['rwkv_mu_f'] = _pack_rwkv_cols(row(P['rwkv_mu_f'][l]))
    W['rwkv_mu_b'] = _pack_rwkv_cols(row(P['rwkv_mu_b'][l]))
    W['rwkv_w0_f'] = row(P['rwkv_w0_f'][l])
    W['rwkv_w2_f'] = _pad_rows(P['rwkv_w2_f'][l], LANE).astype(BF16)
    W['rwkv_w0_b'] = row(P['rwkv_w0_b'][l])
    W['rwkv_w2_b'] = _pad_rows(P['rwkv_w2_b'][l], LANE).astype(BF16)
    W['rwkv_a0'] = row(P['rwkv_a0'][l])
    W['rwkv_a2'] = _pad_rows(P['rwkv_a2'][l], LANE).astype(BF16)
    W['rwkv_g2'] = P['rwkv_g2'][l].astype(BF16)
    for name in ('rwkv_k_k', 'rwkv_k_a', 'rwkv_r_k', 'rwkv_norm_g', 'rwkv_norm_b'):
        W[name] = row(P[name][l])
    for name in ('w_branch_gla', 'w_branch_hgrn', 'w_branch_rwkv', 'w_out', 'xattn_wq', 'xattn_wo'):
        W[name] = P[name][l].astype(BF16)
    W['peer_wq'] = P['peer_wq'][l]
    W['xattn_wkv'] = jnp.concatenate([P['xattn_wk'][l], P['xattn_wv'][l]], axis=1).astype(BF16)
    for name in ('norm_x_g', 'norm_mem_g', 'norm_ffn_g'):
        W[name] = P[name][l]
    half = PEER_DK // 2
    W['peer_sk1'] = jnp.pad(P['peer_subkeys_1'][l], ((0, 0), (0, half)))
    W['peer_sk2'] = jnp.pad(P['peer_subkeys_2'][l], ((0, 0), (half, 0)))
    W['peer_u'] = P['peer_u'][l].astype(BF16)
    W['peer_vt'] = P['peer_v'][l].astype(BF16).T
    return W


def _shared_consts():
    L = RWKV_CHUNK
    idx = jnp.arange(2 * L)
    same = (idx[:, None] // L) == (idx[None, :] // L)
    tpos = idx % L
    C = {
        'gla_ebc': _head_select(GLA_DK, GLA_HEADS),
        'gla_bdt': _head_match(MIX_W, GLA_DV, GLA_W, GLA_DK, F32),
        'hgrn_ebc': _head_select(HGRN_DK, HGRN_HEADS),
        'hgrn_bdt': _head_match(MIX_W, HGRN_DV, HGRN_W, HGRN_DK, F32),
        'hh128': _head_match(MIX_W, 128, MIX_W, 128, BF16),
        'hh64': _head_match(MIX_W, RWKV_HEAD, MIX_W, RWKV_HEAD, BF16),
        'rwkv_bd': _head_match(MIX_W, RWKV_HEAD, MIX_W, RWKV_HEAD, F32),
        'strict_f': (same & (tpos[None, :] < tpos[:, None])).astype(F32),
        'incl_f': (same & (tpos[None, :] <= tpos[:, None])).astype(F32),
        'strict_b': (same & (tpos[None, :] > tpos[:, None])).astype(F32),
        'incl_b': (same & (tpos[None, :] >= tpos[:, None])).astype(F32),
    }
    return C


def _encoder_layer(x, mem, W, C, bsz, t, n_mem):
    g_mix = W['norm_mix_g']
    z_gla = _norm_matmul(x, g_mix, W['w_gla'])
    z_hgrn = _norm_matmul(x, g_mix, W['w_hgrn'])
    z_rwkv = _norm_matmul(x, g_mix, W['w_rwkv'])
    z_gate = _norm_matmul(x, g_mix, W['w_gate'])

    gla_c = (C['gla_ebc'], C['gla_bdt'])
    o_f = _lin_attn_pass(_gla_kernel, z_gla, (W['gla_up_f'], W['gla_bias_f']) + gla_c, None, None,
                         bsz, t, GLA_W, False)
    o_gla = _lin_attn_pass(_gla_kernel, z_gla, (W['gla_up_b'], W['gla_bias_b']) + gla_c, o_f,
                           (C['hh128'], W['gla_norm_g']), bsz, t, GLA_W, True)

    hg_c = (W['hgrn_lb'], C['hgrn_ebc'], C['hgrn_bdt'])
    o_f = _lin_attn_pass(_hgrn_kernel, z_hgrn, hg_c, None, None, bsz, t, HGRN_W, False)
    o_hgrn = _lin_attn_pass(_hgrn_kernel, z_hgrn, hg_c, o_f, (C['hh128'], W['hgrn_norm_g']),
                            bsz, t, HGRN_W, True)

    prep_c = (W['rwkv_mu_f'], W['rwkv_mu_b'], W['rwkv_w0_f'], W['rwkv_w2_f'], W['rwkv_w0_b'],
              W['rwkv_w2_b'], W['rwkv_a0'], W['rwkv_a2'], W['rwkv_g2'], W['rwkv_k_k'], W['rwkv_k_a'],
              W['rwkv_r_k'], C['hh64'])
    r, kh, v, kk, b, lw_f, lw_b, g, bonus = _rwkv_prep(z_rwkv, prep_c, bsz, t)
    o_rf = _rwkv_scan(r, lw_f, kh, v, kk, b, (C['rwkv_bd'], C['strict_f'], C['incl_f']), bsz, t, False)
    o_rb = _rwkv_scan(r, lw_b, kh, v, kk, b, (C['rwkv_bd'], C['strict_b'], C['incl_b']), bsz, t, True)

    x = _merge(x, (o_gla, o_hgrn, o_rf, o_rb, bonus, g, z_gate),
               (W['w_branch_gla'], W['w_branch_hgrn'], W['w_branch_rwkv'], W['w_out'],
                W['rwkv_norm_g'], W['rwkv_norm_b'], C['hh64']))

    kv = _norm_matmul(mem, W['norm_mem_g'], W['xattn_wkv'])
    x = _xattn(x, kv, W['norm_x_g'], W['xattn_wq'], W['xattn_wo'], bsz, t, n_mem)

    q, h = _norm_matmul(x, W['norm_ffn_g'], W['peer_wq'], emit_h=True)
    s1, s2, st = _peer_topk(q, W['peer_sk1'], W['peer_sk2'])
    return _peer_experts(h, s1, s2, st, x, W['peer_u'], W['peer_vt'])


def _run_trunk(x, mem, P, weights, C):
    bsz, t, d = x.shape
    n_mem = mem.shape[1]
    x = x.reshape(bsz * t, d)
    mem = mem.reshape(bsz * n_mem, d)
    for W in weights:
        x = _encoder_layer(x, mem, W, C, bsz, t, n_mem)
    return _final_norm(x, P['final_norm_g']).reshape(bsz, t, d)


def _hgrn_lower_bounds(logits):
    sm = jax.nn.softmax(logits.astype(F32), axis=0)
    return jnp.cumsum(sm, axis=0) - sm[0]


def kernel(x_prompt, x_sample, mem_prompt, mem_sample, norm_mix_g, w_in, gla_gate_up_f, gla_gate_up_b, gla_gate_bias_f, gla_gate_bias_b, gla_norm_g, hgrn_lb_logits, hgrn_norm_g, rwkv_mu_f, rwkv_mu_b, rwkv_w0_f, rwkv_w2_f, rwkv_w0_b, rwkv_w2_b, rwkv_a0, rwkv_a2, rwkv_g2, rwkv_k_k, rwkv_k_a, rwkv_r_k, rwkv_norm_g, rwkv_norm_b, w_branch_gla, w_branch_hgrn, w_branch_rwkv, w_out, norm_x_g, norm_mem_g, xattn_wq, xattn_wk, xattn_wv, xattn_wo, norm_ffn_g, peer_wq, peer_subkeys_1, peer_subkeys_2, peer_u, peer_v, final_norm_g):
    P = dict(norm_mix_g=norm_mix_g, w_in=w_in, gla_gate_up_f=gla_gate_up_f, gla_gate_up_b=gla_gate_up_b,
             gla_gate_bias_f=gla_gate_bias_f, gla_gate_bias_b=gla_gate_bias_b, gla_norm_g=gla_norm_g,
             hgrn_lb_logits=hgrn_lb_logits, hgrn_norm_g=hgrn_norm_g, rwkv_mu_f=rwkv_mu_f, rwkv_mu_b=rwkv_mu_b,
             rwkv_w0_f=rwkv_w0_f, rwkv_w2_f=rwkv_w2_f, rwkv_w0_b=rwkv_w0_b, rwkv_w2_b=rwkv_w2_b,
             rwkv_a0=rwkv_a0, rwkv_a2=rwkv_a2, rwkv_g2=rwkv_g2, rwkv_k_k=rwkv_k_k, rwkv_k_a=rwkv_k_a,
             rwkv_r_k=rwkv_r_k, rwkv_norm_g=rwkv_norm_g, rwkv_norm_b=rwkv_norm_b, w_branch_gla=w_branch_gla,
             w_branch_hgrn=w_branch_hgrn, w_branch_rwkv=w_branch_rwkv, w_out=w_out, norm_x_g=norm_x_g,
             norm_mem_g=norm_mem_g, xattn_wq=xattn_wq, xattn_wk=xattn_wk, xattn_wv=xattn_wv, xattn_wo=xattn_wo,
             norm_ffn_g=norm_ffn_g, peer_wq=peer_wq, peer_subkeys_1=peer_subkeys_1,
             peer_subkeys_2=peer_subkeys_2, peer_u=peer_u, peer_v=peer_v, final_norm_g=final_norm_g)
    depth = w_in.shape[0]
    lbs = _hgrn_lower_bounds(hgrn_lb_logits)
    weights = [_layer_weights(P, l, lbs[l]) for l in range(depth)]
    C = _shared_consts()
    return (_run_trunk(x_prompt, mem_prompt, P, weights, C),
            _run_trunk(x_sample, mem_sample, P, weights, C))
```

```python
import functools

import jax
import jax.numpy as jnp
from jax import lax
from jax.experimental import pallas as pl
from jax.experimental.pallas import tpu as pltpu

F32 = jnp.float32
BF16 = jnp.bfloat16

D_MODEL = 1024
EPS = 1e-6
LOG_FLOOR = 1e-30
DEAD_EXPONENT = -1e30
MIX_W = 512
GLA_HEADS, GLA_DK, GLA_DV = 4, 64, 128
GLA_GATE_RANK = 16
GLA_GATE_NORM = 16.0
HGRN_HEADS, HGRN_DK, HGRN_DV = 4, 128, 128
RWKV_HEADS, RWKV_HEAD = 8, 64
RWKV_DECAY_RANK, RWKV_AAA_RANK, RWKV_GATE_RANK = 64, 64, 128
RWKV_DECAY_SCALE = 0.606531
RWKV_GN_EPS = 64e-5
X_HEADS = 4
X_HEAD = D_MODEL // X_HEADS
PEER_HEADS, PEER_DK, PEER_NKEYS, PEER_TOPK = 8, 128, 128, 16

GLA_W = GLA_HEADS * GLA_DK
GLA_IN = 2 * GLA_W + 2 * MIX_W + 2 * GLA_GATE_RANK
HGRN_W = HGRN_HEADS * HGRN_DK
HGRN_IN = 5 * MIX_W
RWKV_IN = 3 * MIX_W + 2 * RWKV_DECAY_RANK + RWKV_AAA_RANK + RWKV_GATE_RANK
GATE_OFF = GLA_IN + HGRN_IN + RWKV_IN

LANE = 128
SUBLANES = 8
ZGLA_W = 2 * GLA_W + 2 * MIX_W + LANE
ZRWKV_W = 3 * MIX_W + 4 * LANE

ROW_TILE = 512
MERGE_TILE = 256
GLA_CHUNK = 32
GLA_TB = 256
RWKV_CHUNK = 64
RWKV_TB = 256
PREP_TB = 256
XATTN_TQ = 256
TOPK_TB = 128
PEER_TB = 256
PEER_ET = 2048
PEER_MM = 1024
VMEM_LIMIT = 48 * 1024 * 1024


def _params(sem):
    return pltpu.CompilerParams(dimension_semantics=sem, vmem_limit_bytes=VMEM_LIMIT)


def _dot(a, b):
    return jnp.dot(a.astype(BF16), b.astype(BF16), preferred_element_type=F32)


def _dot_nt(a, b):
    return lax.dot_general(a.astype(BF16), b.astype(BF16), (((1,), (1,)), ((), ())),
                           preferred_element_type=F32)


def _dot_nt_f32(a, b):
    return lax.dot_general(a, b, (((1,), (1,)), ((), ())), precision=lax.Precision.HIGHEST,
                           preferred_element_type=F32)


def _dot_tn(a, b):
    return lax.dot_general(a.astype(BF16), b.astype(BF16), (((0,), (0,)), ((), ())),
                           preferred_element_type=F32)


def _split3(x):
    hi = x.astype(BF16)
    r1 = x - hi.astype(F32)
    mid = r1.astype(BF16)
    lo = (r1 - mid.astype(F32)).astype(BF16)
    return hi, mid, lo


def _dot_exact_rhs(x, m):
    hi, mid, lo = _split3(x)
    f = lambda p: jnp.dot(p, m, preferred_element_type=F32)
    return f(hi) + f(mid) + f(lo)


def _dot_exact_lhs(m, x):
    hi, mid, lo = _split3(x)
    f = lambda p: jnp.dot(m, p, preferred_element_type=F32)
    return f(hi) + f(mid) + f(lo)


def _tri_incl(n):
    r = lax.broadcasted_iota(jnp.int32, (n, n), 0)
    c = lax.broadcasted_iota(jnp.int32, (n, n), 1)
    return (r >= c).astype(BF16)


def _rmsnorm(x, g):
    return x * lax.rsqrt(jnp.mean(x * x, axis=-1, keepdims=True) + EPS) * g


def _log_sigmoid(x):
    return jnp.minimum(x, 0.0) - jnp.log(1.0 + jnp.exp(-jnp.abs(x)))


def _silu(x):
    return x * jax.nn.sigmoid(x)


def _norm_matmul_kernel(x_ref, g_ref, w_ref, o_ref, *h_ref):
    h = _rmsnorm(x_ref[...], g_ref[...])
    if w_ref.dtype == F32:
        o_ref[...] = jnp.dot(h, w_ref[...], precision=lax.Precision.HIGHEST, preferred_element_type=F32)
    else:
        o_ref[...] = jnp.dot(h.astype(BF16), w_ref[...], preferred_element_type=F32)
    if h_ref:
        h_ref[0][...] = h.astype(h_ref[0].dtype)


def _norm_matmul(x, g, w, emit_h=False):
    n, d = x.shape
    c = w.shape[1]
    tm = min(ROW_TILE, n)
    out_shape = [jax.ShapeDtypeStruct((n, c), F32)]
    out_specs = [pl.BlockSpec((tm, c), lambda i: (i, 0))]
    if emit_h:
        out_shape.append(jax.ShapeDtypeStruct((n, d), BF16))
        out_specs.append(pl.BlockSpec((tm, d), lambda i: (i, 0)))
    res = pl.pallas_call(
        _norm_matmul_kernel,
        grid=(n // tm,),
        in_specs=[pl.BlockSpec((tm, d), lambda i: (i, 0)),
                  pl.BlockSpec((1, d), lambda i: (0, 0)),
                  pl.BlockSpec((d, c), lambda i: (0, 0))],
        out_specs=out_specs,
        out_shape=out_shape,
        compiler_params=_params(("parallel",)),
    )(x, g.reshape(1, d), w)
    return res if emit_h else res[0]


def _final_norm_kernel(x_ref, g_ref, o_ref):
    o_ref[...] = _rmsnorm(x_ref[...], g_ref[...])


def _final_norm(x, g):
    n, d = x.shape
    tm = min(ROW_TILE, n)
    return pl.pallas_call(
        _final_norm_kernel,
        grid=(n // tm,),
        in_specs=[pl.BlockSpec((tm, d), lambda i: (i, 0)), pl.BlockSpec((1, d), lambda i: (0, 0))],
        out_specs=pl.BlockSpec((tm, d), lambda i: (i, 0)),
        out_shape=jax.ShapeDtypeStruct((n, d), F32),
        compiler_params=_params(("parallel",)),
    )(x, g.reshape(1, d))


def _gla_time_block(q, k, v, g, o_scr, st_scr, qs, ks, vs, cs, ebc_ref, msk_ref, tri_ref, blk_ref, reverse):
    tb, w = q.shape
    hv = v.shape[1]
    C = GLA_CHUNK
    nchunks = tb // C
    cum = _dot_exact_lhs(tri_ref[...], g)
    tot = _dot_exact_lhs(blk_ref[...], g)
    if reverse:
        cum = tot - cum + g
    qs[...] = q
    ks[...] = k
    vs[...] = v
    cs[...] = cum
    pos = lax.broadcasted_iota(jnp.int32, (C, 1), 0)
    heads = hv // LANE
    dk = w // heads
    for h in range(heads):
        lt = (h * dk) // LANE
        ls = slice(lt * LANE, (lt + 1) * LANE)
        hs = slice(h * LANE, (h + 1) * LANE)

        def body(a, o, h=h, ls=ls, hs=hs):
            for b in range(SUBLANES):
                j = a * SUBLANES + b
                live = (pos <= j) if reverse else (pos >= j)
                w_ij, v_j = [], []
                for c in range(nchunks):
                    rs = slice(c * C, (c + 1) * C)
                    k_j = ks[pl.ds(c * C + j, 1), :][:, ls]
                    c_j = cs[pl.ds(c * C + j, 1), :][:, ls]
                    v_j.append(vs[pl.ds(c * C + j, 1), :][:, hs])
                    decay = jnp.exp(jnp.where(live, cs[rs, ls] - c_j, DEAD_EXPONENT))
                    w_ij.append((qs[rs, ls] * k_j * decay).astype(BF16))
                s_j = jnp.dot(jnp.concatenate(w_ij, axis=0), ebc_ref[h], preferred_element_type=F32)
                o = o + jnp.concatenate([s_j[c * C:(c + 1) * C] * v_j[c] for c in range(nchunks)], axis=0)
            return o

        o_scr[:, hs] = lax.fori_loop(0, C // SUBLANES, body, jnp.zeros((tb, LANE), F32))
    qe = q * jnp.exp(cum)
    kd = k * jnp.exp(tot - cum)
    dec = jnp.exp(tot)
    ntile = w // LANE
    vw = hv // ntile
    for c in (range(nchunks - 1, -1, -1) if reverse else range(nchunks)):
        sl = slice(c * C, (c + 1) * C)
        st = [st_scr[p] for p in range(ntile)]
        inter = [_dot_nt(qe[sl, p * LANE:(p + 1) * LANE], st[p]) for p in range(ntile)]
        o_scr[sl, :] = o_scr[sl, :] + jnp.concatenate(inter, axis=1)
        upd = [_dot_tn(v[sl, p * vw:(p + 1) * vw], kd[sl, p * LANE:(p + 1) * LANE]) for p in range(ntile)]
        for p in range(ntile):
            if vw > LANE:
                upd[p] = upd[p] * msk_ref[...]
            st_scr[p] = st[p] * dec[c * C:c * C + 1, p * LANE:(p + 1) * LANE] + upd[p]


def _head_rms_gate(o, og, hh_ref, ng_ref, dv):
    ms = _dot_exact_rhs(o * o, hh_ref[...]) * (1.0 / dv)
    return o * lax.rsqrt(ms + EPS) * ng_ref[...] * _silu(og)


def _gla_kernel(*refs, reverse, final):
    if final:
        (z_ref, up_ref, bias_ref, ebc_ref, msk_ref, tri_ref, blk_ref, oprev_ref, hh_ref, ng_ref,
         o_ref, st_scr, o_scr, qs, ks, vs, cs) = refs
    else:
        (z_ref, up_ref, bias_ref, ebc_ref, msk_ref, tri_ref, blk_ref,
         o_ref, st_scr, o_scr, qs, ks, vs, cs) = refs

    @pl.when(pl.program_id(1) == 0)
    def _():
        st_scr[...] = jnp.zeros_like(st_scr)

    z = z_ref[...]
    q = z[:, 0:GLA_W] * (GLA_DK ** -0.5)
    k = z[:, GLA_W:2 * GLA_W]
    v = z[:, 2 * GLA_W:2 * GLA_W + MIX_W]
    gd = z[:, 2 * GLA_W + 2 * MIX_W:ZGLA_W]
    g = _log_sigmoid(_dot(gd, up_ref[...]) + bias_ref[...]) * (1.0 / GLA_GATE_NORM)
    _gla_time_block(q, k, v, g, o_scr, st_scr, qs, ks, vs, cs, ebc_ref, msk_ref, tri_ref, blk_ref, reverse)
    if final:
        og = z[:, 2 * GLA_W + MIX_W:2 * GLA_W + 2 * MIX_W]
        o_ref[...] = _head_rms_gate(oprev_ref[...] + o_scr[...], og, hh_ref, ng_ref, GLA_DV)
    else:
        o_ref[...] = o_scr[...]


def _hgrn_kernel(*refs, reverse, final):
    if final:
        (z_ref, lb_ref, ebc_ref, msk_ref, tri_ref, blk_ref, oprev_ref, hh_ref, ng_ref,
         o_ref, st_scr, o_scr, qs, ks, vs, cs) = refs
    else:
        z_ref, lb_ref, ebc_ref, msk_ref, tri_ref, blk_ref, o_ref, st_scr, o_scr, qs, ks, vs, cs = refs

    @pl.when(pl.program_id(1) == 0)
    def _():
        st_scr[...] = jnp.zeros_like(st_scr)

    z = z_ref[...]
    lb = lb_ref[...]
    q = _silu(z[:, 0:HGRN_W])
    zf = z[:, (2 if reverse else 1) * HGRN_W:(3 if reverse else 2) * HGRN_W]
    v = z[:, 3 * HGRN_W:4 * HGRN_W]
    f = lb + (1.0 - lb) * jax.nn.sigmoid(zf)
    g = jnp.log(jnp.maximum(f, LOG_FLOOR))
    k = (1.0 - lb) * jax.nn.sigmoid(-zf)
    _gla_time_block(q, k, v, g, o_scr, st_scr, qs, ks, vs, cs, ebc_ref, msk_ref, tri_ref, blk_ref, reverse)
    if final:
        og = z[:, 4 * HGRN_W:5 * HGRN_W]
        o_ref[...] = _head_rms_gate(oprev_ref[...] + o_scr[...], og, hh_ref, ng_ref, HGRN_DV)
    else:
        o_ref[...] = o_scr[...]


def _seq_row_map(nblk, reverse):
    if reverse:
        return lambda b, i: (b * nblk + nblk - 1 - i, 0)
    return lambda b, i: (b * nblk + i, 0)


def _const_map(b, i):
    return (0, 0)


def _lin_attn_pass(kernel, z, consts, o_prev, final_consts, bsz, t, w, reverse):
    n, zc = z.shape
    tb = min(GLA_TB, t)
    nblk = t // tb
    rmap = _seq_row_map(nblk, reverse)
    final = o_prev is not None
    r = jnp.arange(tb)
    same = (r[:, None] // GLA_CHUNK) == (r[None, :] // GLA_CHUNK)
    consts = tuple(consts) + ((same & (r[None, :] <= r[:, None])).astype(BF16), same.astype(BF16))
    args = [z] + list(consts)
    in_specs = [pl.BlockSpec((tb, zc), rmap)]
    in_specs += [pl.BlockSpec(c.shape, lambda b, i, nd=c.ndim: (0,) * nd) for c in consts]
    if final:
        args += [o_prev] + list(final_consts)
        in_specs += [pl.BlockSpec((tb, MIX_W), rmap)]
        in_specs += [pl.BlockSpec(c.shape, _const_map) for c in final_consts]
    return pl.pallas_call(
        functools.partial(kernel, reverse=reverse, final=final),
        grid=(bsz, nblk),
        in_specs=in_specs,
        out_specs=pl.BlockSpec((tb, MIX_W), rmap),
        out_shape=jax.ShapeDtypeStruct((n, MIX_W), F32),
        scratch_shapes=[pltpu.VMEM((w // LANE, MIX_W * LANE // w, LANE), F32), pltpu.VMEM((tb, MIX_W), F32),
                        pltpu.VMEM((tb, w), F32), pltpu.VMEM((tb, w), F32),
                        pltpu.VMEM((tb, MIX_W), F32), pltpu.VMEM((tb, w), F32)],
        compiler_params=_params(("parallel", "arbitrary")),
    )(*args)


def _head_select(dk, heads):
    h = jnp.arange(heads)[:, None]
    lane = ((h * dk) // LANE) * LANE + jnp.arange(LANE)[None, :]
    own = (lane // dk == h).astype(BF16)
    return jnp.broadcast_to(own[:, :, None], (heads, LANE, LANE))


def _head_match(n_rows, row_blk, n_cols, col_blk, dtype):
    r = jnp.arange(n_rows) // row_blk
    c = jnp.arange(n_cols) // col_blk
    return (r[:, None] == c[None, :]).astype(dtype)


def _rwkv_prep_kernel(z_ref, zp_ref, zn_ref, muf_ref, mub_ref, w0f_ref, w2f_ref, w0b_ref, w2b_ref,
                      a0_ref, a2_ref, g2_ref, kk_ref, ka_ref, rk_ref, hh_ref,
                      r_out, kh_out, v_out, kkn_out, b_out, lwf_out, lwb_out, g_out, bonus_out):
    i = pl.program_id(1)
    last = pl.num_programs(1) - 1
    z = z_ref[...]
    tb = z.shape[0]
    rows = lax.broadcasted_iota(jnp.int32, (tb, 1), 0)
    hp = jnp.where(i == 0, 0.0, zp_ref[7:8, :])
    hn = jnp.where(i == last, 0.0, zn_ref[0:1, :])
    prev = jnp.where(rows == 0, hp, pltpu.roll(z, 1, 0))
    nxt = jnp.where(rows == tb - 1, hn, pltpu.roll(z, tb - 1, 0))
    p = z + muf_ref[...] * (prev - z) + mub_ref[...] * (nxt - z)
    r = p[:, 0:MIX_W]
    k = p[:, MIX_W:2 * MIX_W]
    v = p[:, 2 * MIX_W:3 * MIX_W]
    o = 3 * MIX_W
    wdf, wdb, ad, gd = (p[:, o + j * LANE:o + (j + 1) * LANE] for j in range(4))
    lwf = -RWKV_DECAY_SCALE * jax.nn.sigmoid(w0f_ref[...] + _dot(jnp.tanh(wdf), w2f_ref[...]))
    lwb = -RWKV_DECAY_SCALE * jax.nn.sigmoid(w0b_ref[...] + _dot(jnp.tanh(wdb), w2b_ref[...]))
    a = jax.nn.sigmoid(a0_ref[...] + _dot(ad, a2_ref[...]))
    g = _dot(jax.nn.sigmoid(gd), g2_ref[...])
    kk = k * kk_ref[...]
    ss = _dot_exact_rhs(kk * kk, hh_ref[...])
    kk = kk / jnp.maximum(jnp.sqrt(ss), 1e-12)
    kh = k * (1.0 + (a - 1.0) * ka_ref[...])
    bonus = _dot_exact_rhs(r * kh * rk_ref[...], hh_ref[...]) * v
    r_out[...] = r
    kh_out[...] = kh
    v_out[...] = v
    kkn_out[...] = kk
    b_out[...] = kk * a
    lwf_out[...] = lwf
    lwb_out[...] = lwb
    g_out[...] = g
    bonus_out[...] = bonus


def _rwkv_prep(z, consts, bsz, t):
    n, zc = z.shape
    tb = min(PREP_TB, t)
    nblk = t // tb
    hb = tb // 8
    nrow8 = n // 8
    rmap = _seq_row_map(nblk, False)
    pmap = lambda b, i: (jnp.maximum((b * nblk + i) * hb - 1, 0), 0)
    nmap = lambda b, i: (jnp.minimum((b * nblk + i + 1) * hb, nrow8 - 1), 0)
    in_specs = [pl.BlockSpec((tb, zc), rmap), pl.BlockSpec((8, zc), pmap), pl.BlockSpec((8, zc), nmap)]
    in_specs += [pl.BlockSpec(c.shape, _const_map) for c in consts]
    return pl.pallas_call(
        _rwkv_prep_kernel,
        grid=(bsz, nblk),
        in_specs=in_specs,
        out_specs=[pl.BlockSpec((tb, MIX_W), rmap)] * 9,
        out_shape=[jax.ShapeDtypeStruct((n, MIX_W), F32)] * 9,
        compiler_params=_params(("parallel", "parallel")),
    )(z, z, z, *consts)


def _rwkv_scan_kernel(r_ref, lw_ref, kh_ref, v_ref, kk_ref, b_ref, bd_ref, mstrict_ref, mincl_ref,
                      o_ref, s_scr, *, reverse):
    @pl.when(pl.program_id(1) == 0)
    def _():
        s_scr[...] = jnp.zeros_like(s_scr)

    L = RWKV_CHUNK
    tb = r_ref.shape[0]
    nchunks = tb // L
    tri = _tri_incl(L)
    lane = lax.broadcasted_iota(jnp.int32, (1, LANE), 1)
    m0 = (lane < RWKV_HEAD).astype(F32)
    m1 = 1.0 - m0
    eye = (lax.broadcasted_iota(jnp.int32, (2 * L, 2 * L), 0)
           == lax.broadcasted_iota(jnp.int32, (2 * L, 2 * L), 1)).astype(F32)
    mstrict = mstrict_ref[...]
    mincl = mincl_ref[...]
    groups = range(MIX_W // LANE)
    lanes = lambda p: slice(p * LANE, (p + 1) * LANE)
    stack2 = lambda x: jnp.concatenate([x * m0, x * m1], axis=0)

    pre = []
    for c in range(nchunks):
        sl = slice(c * L, (c + 1) * L)
        r, lw, kh, v, kk, b = (ref[sl, :] for ref in (r_ref, lw_ref, kh_ref, v_ref, kk_ref, b_ref))
        cum = _dot_exact_lhs(tri, lw)
        tot = cum[L - 1:L]
        inc = (tot - cum + lw) if reverse else cum
        exc = inc - lw
        mid = inc[L // 2:L // 2 + 1]
        e_pos = jnp.exp(inc - mid)
        e_neg = jnp.exp(mid - inc)
        e_end = jnp.exp(tot - inc)
        pre.append(dict(sl=sl, v=v, rt=r * e_pos, at=-kk * jnp.exp(exc - mid), bt=b * e_neg, kt=kh * e_neg,
                        a_abs=-kk * jnp.exp(exc), r_abs=r * jnp.exp(inc), dec=jnp.exp(tot),
                        wr=jnp.concatenate([b * e_end, kh * e_end], axis=0)))
    cells = [(c, p) for c in range(nchunks) for p in groups]
    gram = {}
    for c, p in cells:
        d = pre[c]
        lhs = jnp.concatenate([stack2(d['at'][:, lanes(p)]), stack2(d['rt'][:, lanes(p)])], axis=0)
        b_p, k_p = d['bt'][:, lanes(p)], d['kt'][:, lanes(p)]
        gram[c, p] = _dot_nt(lhs, jnp.concatenate([b_p, b_p, k_p, k_p], axis=0))
    n_ab = {k: g[0:2 * L, 0:2 * L] * mstrict for k, g in gram.items()}
    a_ak = {k: g[0:2 * L, 2 * L:4 * L] * mstrict for k, g in gram.items()}
    a_rb = {k: g[2 * L:4 * L, 0:2 * L] * mincl for k, g in gram.items()}
    a_rk = {k: g[2 * L:4 * L, 2 * L:4 * L] * mincl for k, g in gram.items()}
    v_bd = {(c, p): stack2(pre[c]['v'][:, lanes(p)]) for c, p in cells}
    x0 = {k: _dot(a_ak[k], v_bd[k]) for k in cells}
    o0 = {k: _dot(a_rk[k], v_bd[k]) for k in cells}
    tinv = {k: eye + n_ab[k] for k in cells}
    pw = dict(n_ab)
    for _ in range(5):
        pw = {k: _dot(pw[k], pw[k]) for k in cells}
        tinv = {k: tinv[k] + _dot(pw[k], tinv[k]) for k in cells}

    fold2 = lambda x: x[0:L] + x[L:2 * L]
    for c in (range(nchunks - 1, -1, -1) if reverse else range(nchunks)):
        d = pre[c]
        s = s_scr[...]
        am = _dot_nt(d['a_abs'], s)
        rm = _dot_nt(d['r_abs'], s)
        us = [_dot(tinv[c, p], stack2(am[:, lanes(p)]) + x0[c, p]) for p in groups]
        o_bd = [_dot(a_rb[c, p], us[p]) + o0[c, p] for p in groups]
        o_ref[d['sl'], :] = rm + jnp.concatenate([fold2(o) for o in o_bd], axis=1)
        u_all = jnp.concatenate([fold2(u) for u in us], axis=1)
        upd = _dot_tn(jnp.concatenate([u_all, d['v']], axis=0), d['wr'])
        s_scr[...] = s * d['dec'] + upd * bd_ref[...]


def _rwkv_scan(r, lw, kh, v, kk, b, consts, bsz, t, reverse):
    n = r.shape[0]
    tb = min(RWKV_TB, t)
    nblk = t // tb
    rmap = _seq_row_map(nblk, reverse)
    row_spec = pl.BlockSpec((tb, MIX_W), rmap)
    return pl.pallas_call(
        functools.partial(_rwkv_scan_kernel, reverse=reverse),
        grid=(bsz, nblk),
        in_specs=[row_spec] * 6 + [pl.BlockSpec(c.shape, _const_map) for c in consts],
        out_specs=row_spec,
        out_shape=jax.ShapeDtypeStruct((n, MIX_W), F32),
        scratch_shapes=[pltpu.VMEM((MIX_W, MIX_W), F32)],
        compiler_params=_params(("parallel", "arbitrary")),
    )(r, lw, kh, v, kk, b, *consts)


def _merge_kernel(x_ref, ogla_ref, ohg_ref, orf_ref, orb_ref, bonus_ref, g_ref, zg_ref,
                  wg_ref, wh_ref, wr_ref, wo_ref, ng_ref, nb_ref, hh_ref, o_ref):
    o = orf_ref[...] + orb_ref[...]
    inv = 1.0 / RWKV_HEAD
    mean = _dot_exact_rhs(o, hh_ref[...]) * inv
    d = o - mean
    var = _dot_exact_rhs(d * d, hh_ref[...]) * inv
    on = d * lax.rsqrt(var + RWKV_GN_EPS) * ng_ref[...] + nb_ref[...]
    orw = (on + bonus_ref[...]) * g_ref[...]
    zg = zg_ref[...]
    gate = lambda j: jax.nn.sigmoid(zg[:, j * D_MODEL:(j + 1) * D_MODEL])
    merged = (gate(0) * _dot(ogla_ref[...], wg_ref[...])
              + gate(1) * _dot(ohg_ref[...], wh_ref[...])
              + gate(2) * _dot(orw, wr_ref[...]))
    o_ref[...] = x_ref[...] + _dot(merged, wo_ref[...])


def _merge(x, rows, consts):
    n, d = x.shape
    tm = min(MERGE_TILE, n)
    rspec = lambda a: pl.BlockSpec((tm, a.shape[1]), lambda i: (i, 0))
    return pl.pallas_call(
        _merge_kernel,
        grid=(n // tm,),
        in_specs=[rspec(x)] + [rspec(a) for a in rows]
        + [pl.BlockSpec(c.shape, lambda i: (0, 0)) for c in consts],
        out_specs=pl.BlockSpec((tm, d), lambda i: (i, 0)),
        out_shape=jax.ShapeDtypeStruct((n, d), F32),
        compiler_params=_params(("parallel",)),
    )(x, *rows, *consts)


def _xattn_kernel(x_ref, kv_ref, g_ref, wq_ref, wo_ref, o_ref):
    x = x_ref[...]
    h = _rmsnorm(x, g_ref[...])
    q = _dot(h, wq_ref[...])
    kv = kv_ref[...]
    outs = []
    for hd in range(X_HEADS):
        ls = slice(hd * X_HEAD, (hd + 1) * X_HEAD)
        s = _dot_nt(q[:, ls], kv[:, ls]) * (X_HEAD ** -0.5)
        s = s - jnp.max(s, axis=-1, keepdims=True)
        e = jnp.exp(s)
        pr = e / jnp.sum(e, axis=-1, keepdims=True)
        outs.append(_dot(pr, kv[:, D_MODEL + hd * X_HEAD:D_MODEL + (hd + 1) * X_HEAD]))
    o_ref[...] = x + _dot(jnp.concatenate(outs, axis=1), wo_ref[...])


def _xattn(x, kv, g, wq, wo, bsz, t, n_mem):
    n, d = x.shape
    tq = min(XATTN_TQ, t)
    nblk = t // tq
    return pl.pallas_call(
        _xattn_kernel,
        grid=(bsz, nblk),
        in_specs=[pl.BlockSpec((tq, d), lambda b, i: (b * nblk + i, 0)),
                  pl.BlockSpec((n_mem, 2 * d), lambda b, i: (b, 0)),
                  pl.BlockSpec((1, d), _const_map),
                  pl.BlockSpec((d, d), _const_map),
                  pl.BlockSpec((d, d), _const_map)],
        out_specs=pl.BlockSpec((tq, d), lambda b, i: (b * nblk + i, 0)),
        out_shape=jax.ShapeDtypeStruct((n, d), F32),
        compiler_params=_params(("parallel", "parallel")),
    )(x, kv, g.reshape(1, d), wq, wo)


def _top16_rows(x):
    nrow = x.shape[0]
    rows = lax.broadcasted_iota(jnp.int32, x.shape, 0)
    vals = []
    for _ in range(PEER_TOPK):
        m = jnp.max(x, axis=0, keepdims=True)
        pos = jnp.min(jnp.where(x == m, rows, nrow), axis=0, keepdims=True)
        vals.append(m)
        x = jnp.where(rows == pos, -jnp.inf, x)
    return jnp.concatenate(vals, 0)


def _peer_topk_kernel(q_ref, sk1_ref, sk2_ref, s1_ref, s2_ref, st_ref):
    q = q_ref[...]
    thr, mx1, mx2, rz = [], [], [], []
    for hd in range(PEER_HEADS):
        hs = slice(hd * PEER_NKEYS, (hd + 1) * PEER_NKEYS)
        qh = q[:, hd * PEER_DK:(hd + 1) * PEER_DK]
        s1 = _dot_nt_f32(sk1_ref[...], qh)
        s2 = _dot_nt_f32(sk2_ref[...], qh)
        s1_ref[hs, :] = s1
        s2_ref[hs, :] = s2
        v1 = _top16_rows(s1)
        v2 = _top16_rows(s2)
        half = PEER_TOPK // 2
        cand = jnp.concatenate([v1[0:1] + v2] + [v1[a:a + 1] + v2[0:half] for a in range(1, half)]
                               + [v1[half:] + v2[0:1]], axis=0)
        top = _top16_rows(cand)
        thr.append(top[PEER_TOPK - 1:PEER_TOPK])
        mx1.append(v1[0:1])
        mx2.append(v2[0:1])
        rz.append(1.0 / jnp.sum(jnp.exp(top - top[0:1]), axis=0, keepdims=True))
    st_ref[...] = jnp.concatenate(thr + mx1 + mx2 + rz, axis=0)


def _peer_topk(q, sk1p, sk2p):
    n, d = q.shape
    tb = min(TOPK_TB, n)
    nrow = PEER_HEADS * PEER_NKEYS
    col = lambda r: pl.BlockSpec((r, tb), lambda i: (0, i))
    return pl.pallas_call(
        _peer_topk_kernel,
        grid=(n // tb,),
        in_specs=[pl.BlockSpec((tb, d), lambda i: (i, 0)),
                  pl.BlockSpec(sk1p.shape, lambda i: (0, 0)),
                  pl.BlockSpec(sk2p.shape, lambda i: (0, 0))],
        out_specs=[col(nrow), col(nrow), col(4 * PEER_HEADS)],
        out_shape=[jax.ShapeDtypeStruct((nrow, n), F32), jax.ShapeDtypeStruct((nrow, n), F32),
                   jax.ShapeDtypeStruct((4 * PEER_HEADS, n), F32)],
        compiler_params=_params(("parallel",)),
    )(q, sk1p, sk2p)


def _peer_expert_kernel(h_ref, u_ref, vt_ref, s1_ref, s2_ref, st_ref, x_ref, o_ref,
                        acc_scr, e1_scr, e2_scr):
    j = pl.program_id(1)
    nh, nk = PEER_HEADS, PEER_NKEYS

    @pl.when(j == 0)
    def _():
        acc_scr[...] = jnp.zeros_like(acc_scr)
        for hd in range(nh):
            hs = slice(hd * nk, (hd + 1) * nk)
            e1_scr[hs, :] = jnp.exp(s1_ref[hs, :] - st_ref[nh + hd:nh + hd + 1, :])
            e2_scr[hs, :] = (jnp.exp(s2_ref[hs, :] - st_ref[2 * nh + hd:2 * nh + hd + 1, :])
                             * st_ref[3 * nh + hd:3 * nh + hd + 1, :])

    et, tb = u_ref.shape[0], h_ref.shape[0]
    sub = PEER_MM // nk
    for sl in range(et // PEER_MM):
        rows = slice(sl * PEER_MM, (sl + 1) * PEER_MM)
        act = _dot_nt(u_ref[rows, :], h_ref[...])
        coef = []
        for il in range(sub):
            i1 = (j * (et // PEER_MM) + sl) * sub + il
            parts = []
            s1_rows = [s1_ref[pl.ds(hd * nk + i1, 1), :] for hd in range(nh)]
            e1_rows = [e1_scr[pl.ds(hd * nk + i1, 1), :] for hd in range(nh)]
            for lt in range(tb // LANE):
                ts = slice(lt * LANE, (lt + 1) * LANE)
                gate = jnp.zeros((nk, LANE), F32)
                for hd in range(nh):
                    hs = slice(hd * nk, (hd + 1) * nk)
                    score = s1_rows[hd][:, ts] + s2_ref[hs, ts]
                    weight = e1_rows[hd][:, ts] * e2_scr[hs, ts]
                    gate = gate + jnp.where(score >= st_ref[hd:hd + 1, ts], weight, 0.0)
                a = act[il * nk:(il + 1) * nk, ts]
                gelu = 0.5 * a * (1.0 + lax.erf(a * (2.0 ** -0.5)))
                parts.append((gate * gelu).astype(BF16))
            coef.append(jnp.concatenate(parts, axis=1))
        acc_scr[...] += jnp.dot(vt_ref[:, rows], jnp.concatenate(coef, axis=0),
                                preferred_element_type=F32)

    @pl.when(j == pl.num_programs(1) - 1)
    def _():
        o_ref[...] = x_ref[...] + acc_scr[...].T


def _peer_experts(h, s1, s2, st, x, u, vt):
    n, d = x.shape
    ne = u.shape[0]
    tb = min(PEER_TB, n)
    et = PEER_ET
    nrow = s1.shape[0]
    col = lambda r: pl.BlockSpec((r, tb), lambda i, j: (0, i))
    row = pl.BlockSpec((tb, d), lambda i, j: (i, 0))
    return pl.pallas_call(
        _peer_expert_kernel,
        grid=(n // tb, ne // et),
        in_specs=[row, pl.BlockSpec((et, d), lambda i, j: (j, 0)),
                  pl.BlockSpec((d, et), lambda i, j: (0, j)),
                  col(nrow), col(nrow), col(st.shape[0]), row],
        out_specs=row,
        out_shape=jax.ShapeDtypeStruct((n, d), F32),
        scratch_shapes=[pltpu.VMEM((d, tb), F32), pltpu.VMEM((nrow, tb), F32),
                        pltpu.VMEM((nrow, tb), F32)],
        compiler_params=_params(("parallel", "arbitrary")),
    )(h, u, vt, s1, s2, st, x)


def _pad_cols(a, width):
    return jnp.pad(a, ((0, 0), (0, width - a.shape[1])))


def _pad_rows(a, height):
    return jnp.pad(a, ((0, height - a.shape[0]), (0, 0)))


def _pack_rwkv_cols(a):
    o = 3 * MIX_W
    wf = a[:, o:o + RWKV_DECAY_RANK]
    wb = a[:, o + RWKV_DECAY_RANK:o + 2 * RWKV_DECAY_RANK]
    ad = a[:, o + 2 * RWKV_DECAY_RANK:o + 2 * RWKV_DECAY_RANK + RWKV_AAA_RANK]
    gd = a[:, o + 2 * RWKV_DECAY_RANK + RWKV_AAA_RANK:]
    return jnp.concatenate([a[:, :o], _pad_cols(wf, LANE), _pad_cols(wb, LANE), _pad_cols(ad, LANE),
                            _pad_cols(gd, LANE)], axis=1)


def _layer_weights(P, l, lb):
    w_in = P['w_in'][l]
    row = lambda a: a.reshape(1, -1).astype(F32)
    W = {}
    W['norm_mix_g'] = P['norm_mix_g'][l]
    W['w_gla'] = _pad_cols(w_in[:, :GLA_IN], ZGLA_W).astype(BF16)
    W['w_hgrn'] = w_in[:, GLA_IN:GLA_IN + HGRN_IN].astype(BF16)
    W['w_rwkv'] = _pack_rwkv_cols(w_in[:, GLA_IN + HGRN_IN:GATE_OFF]).astype(BF16)
    W['w_gate'] = w_in[:, GATE_OFF:].astype(BF16)
    W['gla_up_f'] = _pad_rows(P['gla_gate_up_f'][l], LANE).astype(BF16)
    W['gla_up_b'] = _pad_rows(jnp.concatenate(
        [jnp.zeros_like(P['gla_gate_up_b'][l]), P['gla_gate_up_b'][l]], axis=0), LANE).astype(BF16)
    W['gla_bias_f'] = row(P['gla_gate_bias_f'][l])
    W['gla_bias_b'] = row(P['gla_gate_bias_b'][l])
    W['gla_norm_g'] = row(P['gla_norm_g'][l])
    W['hgrn_lb'] = row(lb)
    W['hgrn_norm_g'] = row(P['hgrn_norm_g'][l])
    W['rwkv_mu_f'] = _pack_rwkv_cols(row(P['rwkv_mu_f'][l]))
    W['rwkv_mu_b'] = _pack_rwkv_cols(row(P['rwkv_mu_b'][l]))
    W['rwkv_w0_f'] = row(P['rwkv_w0_f'][l])
    W['rwkv_w2_f'] = _pad_rows(P['rwkv_w2_f'][l], LANE).astype(BF16)
    W['rwkv_w0_b'] = row(P['rwkv_w0_b'][l])
    W['rwkv_w2_b'] = _pad_rows(P['rwkv_w2_b'][l], LANE).astype(BF16)
    W['rwkv_a0'] = row(P['rwkv_a0'][l])
    W['rwkv_a2'] = _pad_rows(P['rwkv_a2'][l], LANE).astype(BF16)
    W['rwkv_g2'] = P['rwkv_g2'][l].astype(BF16)
    for name in ('rwkv_k_k', 'rwkv_k_a', 'rwkv_r_k', 'rwkv_norm_g', 'rwkv_norm_b'):
        W[name] = row(P[name][l])
    for name in ('w_branch_gla', 'w_branch_hgrn', 'w_branch_rwkv', 'w_out', 'xattn_wq', 'xattn_wo'):
        W[name] = P[name][l].astype(BF16)
    W['peer_wq'] = P['peer_wq'][l]
    W['xattn_wkv'] = jnp.concatenate([P['xattn_wk'][l], P['xattn_wv'][l]], axis=1).astype(BF16)
    for name in ('norm_x_g', 'norm_mem_g', 'norm_ffn_g'):
        W[name] = P[name][l]
    half = PEER_DK // 2
    W['peer_sk1'] = jnp.pad(P['peer_subkeys_1'][l], ((0, 0), (0, half)))
    W['peer_sk2'] = jnp.pad(P['peer_subkeys_2'][l], ((0, 0), (half, 0)))
    W['peer_u'] = P['peer_u'][l].astype(BF16)
    W['peer_vt'] = P['peer_v'][l].astype(BF16).T
    return W


def _shared_consts():
    L = RWKV_CHUNK
    idx = jnp.arange(2 * L)
    same = (idx[:, None] // L) == (idx[None, :] // L)
    tpos = idx % L
    C = {
        'gla_ebc': _head_select(GLA_DK, GLA_HEADS),
        'gla_bdt': _head_match(MIX_W * LANE // GLA_W, GLA_DV, LANE, GLA_DK, F32),
        'hgrn_ebc': _head_select(HGRN_DK, HGRN_HEADS),
        'hgrn_bdt': _head_match(MIX_W * LANE // HGRN_W, HGRN_DV, LANE, HGRN_DK, F32),
        'hh128': _head_match(MIX_W, 128, MIX_W, 128, BF16),
        'hh64': _head_match(MIX_W, RWKV_HEAD, MIX_W, RWKV_HEAD, BF16),
        'rwkv_bd': _head_match(MIX_W, RWKV_HEAD, MIX_W, RWKV_HEAD, F32),
        'strict_f': (same & (tpos[None, :] < tpos[:, None])).astype(F32),
        'incl_f': (same & (tpos[None, :] <= tpos[:, None])).astype(F32),
        'strict_b': (same & (tpos[None, :] > tpos[:, None])).astype(F32),
        'incl_b': (same & (tpos[None, :] >= tpos[:, None])).astype(F32),
    }
    return C


def _encoder_layer(x, mem, W, C, bsz, t, n_mem):
    g_mix = W['norm_mix_g']
    z_gla = _norm_matmul(x, g_mix, W['w_gla'])
    z_hgrn = _norm_matmul(x, g_mix, W['w_hgrn'])
    z_rwkv = _norm_matmul(x, g_mix, W['w_rwkv'])
    z_gate = _norm_matmul(x, g_mix, W['w_gate'])

    gla_c = (C['gla_ebc'], C['gla_bdt'])
    o_f = _lin_attn_pass(_gla_kernel, z_gla, (W['gla_up_f'], W['gla_bias_f']) + gla_c, None, None,
                         bsz, t, GLA_W, False)
    o_gla = _lin_attn_pass(_gla_kernel, z_gla, (W['gla_up_b'], W['gla_bias_b']) + gla_c, o_f,
                           (C['hh128'], W['gla_norm_g']), bsz, t, GLA_W, True)

    hg_c = (W['hgrn_lb'], C['hgrn_ebc'], C['hgrn_bdt'])
    o_f = _lin_attn_pass(_hgrn_kernel, z_hgrn, hg_c, None, None, bsz, t, HGRN_W, False)
    o_hgrn = _lin_attn_pass(_hgrn_kernel, z_hgrn, hg_c, o_f, (C['hh128'], W['hgrn_norm_g']),
                            bsz, t, HGRN_W, True)

    prep_c = (W['rwkv_mu_f'], W['rwkv_mu_b'], W['rwkv_w0_f'], W['rwkv_w2_f'], W['rwkv_w0_b'],
              W['rwkv_w2_b'], W['rwkv_a0'], W['rwkv_a2'], W['rwkv_g2'], W['rwkv_k_k'], W['rwkv_k_a'],
              W['rwkv_r_k'], C['hh64'])
    r, kh, v, kk, b, lw_f, lw_b, g, bonus = _rwkv_prep(z_rwkv, prep_c, bsz, t)
    o_rf = _rwkv_scan(r, lw_f, kh, v, kk, b, (C['rwkv_bd'], C['strict_f'], C['incl_f']), bsz, t, False)
    o_rb = _rwkv_scan(r, lw_b, kh, v, kk, b, (C['rwkv_bd'], C['strict_b'], C['incl_b']), bsz, t, True)

    x = _merge(x, (o_gla, o_hgrn, o_rf, o_rb, bonus, g, z_gate),
               (W['w_branch_gla'], W['w_branch_hgrn'], W['w_branch_rwkv'], W['w_out'],
                W['rwkv_norm_g'], W['rwkv_norm_b'], C['hh64']))

    kv = _norm_matmul(mem, W['norm_mem_g'], W['xattn_wkv'])
    x = _xattn(x, kv, W['norm_x_g'], W['xattn_wq'], W['xattn_wo'], bsz, t, n_mem)

    q, h = _norm_matmul(x, W['norm_ffn_g'], W['peer_wq'], emit_h=True)
    s1, s2, st = _peer_topk(q, W['peer_sk1'], W['peer_sk2'])
    return _peer_experts(h, s1, s2, st, x, W['peer_u'], W['peer_vt'])


def _run_trunk(x, mem, P, weights, C):
    bsz, t, d = x.shape
    n_mem = mem.shape[1]
    x = x.reshape(bsz * t, d)
    mem = mem.reshape(bsz * n_mem, d)
    for W in weights:
        x = _encoder_layer(x, mem, W, C, bsz, t, n_mem)
    return _final_norm(x, P['final_norm_g']).reshape(bsz, t, d)


def _hgrn_lower_bounds(logits):
    sm = jax.nn.softmax(logits.astype(F32), axis=0)
    return jnp.cumsum(sm, axis=0) - sm[0]


def kernel(x_prompt, x_sample, mem_prompt, mem_sample, norm_mix_g, w_in, gla_gate_up_f, gla_gate_up_b, gla_gate_bias_f, gla_gate_bias_b, gla_norm_g, hgrn_lb_logits, hgrn_norm_g, rwkv_mu_f, rwkv_mu_b, rwkv_w0_f, rwkv_w2_f, rwkv_w0_b, rwkv_w2_b, rwkv_a0, rwkv_a2, rwkv_g2, rwkv_k_k, rwkv_k_a, rwkv_r_k, rwkv_norm_g, rwkv_norm_b, w_branch_gla, w_branch_hgrn, w_branch_rwkv, w_out, norm_x_g, norm_mem_g, xattn_wq, xattn_wk, xattn_wv, xattn_wo, norm_ffn_g, peer_wq, peer_subkeys_1, peer_subkeys_2, peer_u, peer_v, final_norm_g):
    P = dict(norm_mix_g=norm_mix_g, w_in=w_in, gla_gate_up_f=gla_gate_up_f, gla_gate_up_b=gla_gate_up_b,
             gla_gate_bias_f=gla_gate_bias_f, gla_gate_bias_b=gla_gate_bias_b, gla_norm_g=gla_norm_g,
             hgrn_lb_logits=hgrn_lb_logits, hgrn_norm_g=hgrn_norm_g, rwkv_mu_f=rwkv_mu_f, rwkv_mu_b=rwkv_mu_b,
             rwkv_w0_f=rwkv_w0_f, rwkv_w2_f=rwkv_w2_f, rwkv_w0_b=rwkv_w0_b, rwkv_w2_b=rwkv_w2_b,
             rwkv_a0=rwkv_a0, rwkv_a2=rwkv_a2, rwkv_g2=rwkv_g2, rwkv_k_k=rwkv_k_k, rwkv_k_a=rwkv_k_a,
             rwkv_r_k=rwkv_r_k, rwkv_norm_g=rwkv_norm_g, rwkv_norm_b=rwkv_norm_b, w_branch_gla=w_branch_gla,
             w_branch_hgrn=w_branch_hgrn, w_branch_rwkv=w_branch_rwkv, w_out=w_out, norm_x_g=norm_x_g,
             norm_mem_g=norm_mem_g, xattn_wq=xattn_wq, xattn_wk=xattn_wk, xattn_wv=xattn_wv, xattn_wo=xattn_wo,
             norm_ffn_g=norm_ffn_g, peer_wq=peer_wq, peer_subkeys_1=peer_subkeys_1,
             peer_subkeys_2=peer_subkeys_2, peer_u=peer_u, peer_v=peer_v, final_norm_g=final_norm_g)
    depth = w_in.shape[0]
    lbs = _hgrn_lower_bounds(hgrn_lb_logits)
    weights = [_layer_weights(P, l, lbs[l]) for l in range(depth)]
    C = _shared_consts()
    return (_run_trunk(x_prompt, mem_prompt, P, weights, C),
            _run_trunk(x_sample, mem_sample, P, weights, C))
```

```python
import functools

import jax
import jax.numpy as jnp
from jax import lax
from jax.experimental import pallas as pl
from jax.experimental.pallas import tpu as pltpu

F32 = jnp.float32
BF16 = jnp.bfloat16

D_MODEL = 1024
EPS = 1e-6
LOG_FLOOR = 1e-30
DEAD_EXPONENT = -1e30
MIX_W = 512
GLA_HEADS, GLA_DK, GLA_DV = 4, 64, 128
GLA_GATE_RANK = 16
GLA_GATE_NORM = 16.0
HGRN_HEADS, HGRN_DK, HGRN_DV = 4, 128, 128
RWKV_HEADS, RWKV_HEAD = 8, 64
RWKV_DECAY_RANK, RWKV_AAA_RANK, RWKV_GATE_RANK = 64, 64, 128
RWKV_DECAY_SCALE = 0.606531
RWKV_GN_EPS = 64e-5
X_HEADS = 4
X_HEAD = D_MODEL // X_HEADS
PEER_HEADS, PEER_DK, PEER_NKEYS, PEER_TOPK = 8, 128, 128, 16

GLA_W = GLA_HEADS * GLA_DK
GLA_IN = 2 * GLA_W + 2 * MIX_W + 2 * GLA_GATE_RANK
HGRN_W = HGRN_HEADS * HGRN_DK
HGRN_IN = 5 * MIX_W
RWKV_IN = 3 * MIX_W + 2 * RWKV_DECAY_RANK + RWKV_AAA_RANK + RWKV_GATE_RANK
GATE_OFF = GLA_IN + HGRN_IN + RWKV_IN

LANE = 128
SUBLANES = 8
ZGLA_W = 2 * GLA_W + 2 * MIX_W + LANE
ZRWKV_W = 3 * MIX_W + 4 * LANE

ROW_TILE = 512
MERGE_TILE = 256
GLA_CHUNK = 32
GLA_TB = 256
RWKV_CHUNK = 64
RWKV_TB = 256
PREP_TB = 256
XATTN_TQ = 256
TOPK_TB = 128
PEER_TB = 256
PEER_ET = 2048
PEER_MM = 1024
VMEM_LIMIT = 48 * 1024 * 1024


def _params(sem):
    return pltpu.CompilerParams(dimension_semantics=sem, vmem_limit_bytes=VMEM_LIMIT)


def _dot(a, b):
    return jnp.dot(a.astype(BF16), b.astype(BF16), preferred_element_type=F32)


def _dot_nt(a, b):
    return lax.dot_general(a.astype(BF16), b.astype(BF16), (((1,), (1,)), ((), ())),
                           preferred_element_type=F32)


def _dot_nt_f32(a, b):
    return lax.dot_general(a, b, (((1,), (1,)), ((), ())), precision=lax.Precision.HIGHEST,
                           preferred_element_type=F32)


def _dot_tn(a, b):
    return lax.dot_general(a.astype(BF16), b.astype(BF16), (((0,), (0,)), ((), ())),
                           preferred_element_type=F32)


def _split3(x):
    hi = x.astype(BF16)
    r1 = x - hi.astype(F32)
    mid = r1.astype(BF16)
    lo = (r1 - mid.astype(F32)).astype(BF16)
    return hi, mid, lo


def _dot_exact_rhs(x, m):
    hi, mid, lo = _split3(x)
    f = lambda p: jnp.dot(p, m, preferred_element_type=F32)
    return f(hi) + f(mid) + f(lo)


def _dot_exact_lhs(m, x):
    hi, mid, lo = _split3(x)
    f = lambda p: jnp.dot(m, p, preferred_element_type=F32)
    return f(hi) + f(mid) + f(lo)


def _tri_incl(n):
    r = lax.broadcasted_iota(jnp.int32, (n, n), 0)
    c = lax.broadcasted_iota(jnp.int32, (n, n), 1)
    return (r >= c).astype(BF16)


def _rmsnorm(x, g):
    return x * lax.rsqrt(jnp.mean(x * x, axis=-1, keepdims=True) + EPS) * g


def _log_sigmoid(x):
    return jnp.minimum(x, 0.0) - jnp.log(1.0 + jnp.exp(-jnp.abs(x)))


def _silu(x):
    return x * jax.nn.sigmoid(x)


def _norm_matmul_kernel(x_ref, g_ref, w_ref, o_ref, *h_ref):
    h = _rmsnorm(x_ref[...], g_ref[...])
    if w_ref.dtype == F32:
        o_ref[...] = jnp.dot(h, w_ref[...], precision=lax.Precision.HIGHEST, preferred_element_type=F32)
    else:
        o_ref[...] = jnp.dot(h.astype(BF16), w_ref[...], preferred_element_type=F32)
    if h_ref:
        h_ref[0][...] = h.astype(h_ref[0].dtype)


def _norm_matmul(x, g, w, emit_h=False):
    n, d = x.shape
    c = w.shape[1]
    tm = min(ROW_TILE, n)
    out_shape = [jax.ShapeDtypeStruct((n, c), F32)]
    out_specs = [pl.BlockSpec((tm, c), lambda i: (i, 0))]
    if emit_h:
        out_shape.append(jax.ShapeDtypeStruct((n, d), BF16))
        out_specs.append(pl.BlockSpec((tm, d), lambda i: (i, 0)))
    res = pl.pallas_call(
        _norm_matmul_kernel,
        grid=(n // tm,),
        in_specs=[pl.BlockSpec((tm, d), lambda i: (i, 0)),
                  pl.BlockSpec((1, d), lambda i: (0, 0)),
                  pl.BlockSpec((d, c), lambda i: (0, 0))],
        out_specs=out_specs,
        out_shape=out_shape,
        compiler_params=_params(("parallel",)),
    )(x, g.reshape(1, d), w)
    return res if emit_h else res[0]


def _final_norm_kernel(x_ref, g_ref, o_ref):
    o_ref[...] = _rmsnorm(x_ref[...], g_ref[...])


def _final_norm(x, g):
    n, d = x.shape
    tm = min(ROW_TILE, n)
    return pl.pallas_call(
        _final_norm_kernel,
        grid=(n // tm,),
        in_specs=[pl.BlockSpec((tm, d), lambda i: (i, 0)), pl.BlockSpec((1, d), lambda i: (0, 0))],
        out_specs=pl.BlockSpec((tm, d), lambda i: (i, 0)),
        out_shape=jax.ShapeDtypeStruct((n, d), F32),
        compiler_params=_params(("parallel",)),
    )(x, g.reshape(1, d))


def _gla_time_block(q, k, v, g, o_scr, st_scr, qs, ks, vs, cs, ebc_ref, msk_ref, tri_ref, blk_ref, sel_ref,
                    reverse):
    tb, w = q.shape
    hv = v.shape[1]
    C, S = GLA_CHUNK, SUBLANES
    nchunks, nsub = tb // C, C // S
    cum = _dot_exact_lhs(tri_ref[...], g)
    tot = _dot_exact_lhs(blk_ref[...], g)
    if reverse:
        cum = tot - cum + g
    qs[...] = q
    ks[...] = k
    vs[...] = v
    cs[...] = cum
    sub = (lax.broadcasted_iota(jnp.int32, (tb, 1), 0) % C) // S
    qf, kf = [], []
    for bd in range(1, nsub):
        ref = _dot_exact_lhs(sel_ref[bd - 1], cum)
        if reverse:
            q_rows, k_rows = sub == bd - 1, sub >= bd
        else:
            q_rows, k_rows = sub == bd, sub < bd
        qf.append(q * jnp.exp(jnp.where(q_rows, cum - ref, DEAD_EXPONENT)))
        kf.append(k * jnp.exp(jnp.where(k_rows, ref - cum, DEAD_EXPONENT)))
    same_chunk = blk_ref[...].astype(F32)
    pos = lax.broadcasted_iota(jnp.int32, (S, 1), 0)
    lane = lax.broadcasted_iota(jnp.int32, (1, LANE), 1)
    heads = hv // LANE
    dk = w // heads
    for h in range(heads):
        lt = (h * dk) // LANE
        ls = slice(lt * LANE, (lt + 1) * LANE)
        hs = slice(h * LANE, (h + 1) * LANE)
        o = jnp.zeros((tb, LANE), F32)
        for b in range(S):
            live = (pos <= b) if reverse else (pos >= b)
            w_ij, v_j = [], []
            for sb in range(tb // S):
                rs = slice(sb * S, (sb + 1) * S)
                row = slice(sb * S + b, sb * S + b + 1)
                v_j.append(vs[row, hs])
                decay = jnp.exp(jnp.where(live, cs[rs, ls] - cs[row, ls], DEAD_EXPONENT))
                w_ij.append((qs[rs, ls] * ks[row, ls] * decay).astype(BF16))
            s_j = jnp.dot(jnp.concatenate(w_ij, axis=0), ebc_ref[h], preferred_element_type=F32)
            o = o + jnp.concatenate([s_j[sb * S:(sb + 1) * S] * v_j[sb] for sb in range(tb // S)], axis=0)
        own = ((lane // dk) == (h % (LANE // dk))).astype(F32) if dk < LANE else None
        scores = jnp.zeros((tb, tb), F32)
        for bd in range(nsub - 1):
            q_h = qf[bd][:, ls] if own is None else qf[bd][:, ls] * own
            scores = scores + _dot_nt(q_h, kf[bd][:, ls])
        o_scr[:, hs] = o + _dot(scores * same_chunk, v[:, hs])
    qe = q * jnp.exp(cum)
    kd = k * jnp.exp(tot - cum)
    dec = jnp.exp(tot)
    ntile = w // LANE
    vw = hv // ntile
    for c in (range(nchunks - 1, -1, -1) if reverse else range(nchunks)):
        sl = slice(c * C, (c + 1) * C)
        st = [st_scr[p] for p in range(ntile)]
        inter = [_dot_nt(qe[sl, p * LANE:(p + 1) * LANE], st[p]) for p in range(ntile)]
        o_scr[sl, :] = o_scr[sl, :] + jnp.concatenate(inter, axis=1)
        upd = [_dot_tn(v[sl, p * vw:(p + 1) * vw], kd[sl, p * LANE:(p + 1) * LANE]) for p in range(ntile)]
        for p in range(ntile):
            if vw > LANE:
                upd[p] = upd[p] * msk_ref[...]
            st_scr[p] = st[p] * dec[c * C:c * C + 1, p * LANE:(p + 1) * LANE] + upd[p]


def _head_rms_gate(o, og, hh_ref, ng_ref, dv):
    ms = _dot_exact_rhs(o * o, hh_ref[...]) * (1.0 / dv)
    return o * lax.rsqrt(ms + EPS) * ng_ref[...] * _silu(og)


def _gla_kernel(*refs, reverse, final):
    if final:
        (z_ref, up_ref, bias_ref, ebc_ref, msk_ref, tri_ref, blk_ref, sel_ref, oprev_ref, hh_ref, ng_ref,
         o_ref, st_scr, o_scr, qs, ks, vs, cs) = refs
    else:
        (z_ref, up_ref, bias_ref, ebc_ref, msk_ref, tri_ref, blk_ref, sel_ref,
         o_ref, st_scr, o_scr, qs, ks, vs, cs) = refs

    @pl.when(pl.program_id(1) == 0)
    def _():
        st_scr[...] = jnp.zeros_like(st_scr)

    z = z_ref[...]
    q = z[:, 0:GLA_W] * (GLA_DK ** -0.5)
    k = z[:, GLA_W:2 * GLA_W]
    v = z[:, 2 * GLA_W:2 * GLA_W + MIX_W]
    gd = z[:, 2 * GLA_W + 2 * MIX_W:ZGLA_W]
    g = _log_sigmoid(_dot(gd, up_ref[...]) + bias_ref[...]) * (1.0 / GLA_GATE_NORM)
    _gla_time_block(q, k, v, g, o_scr, st_scr, qs, ks, vs, cs, ebc_ref, msk_ref, tri_ref, blk_ref, sel_ref,
                    reverse)
    if final:
        og = z[:, 2 * GLA_W + MIX_W:2 * GLA_W + 2 * MIX_W]
        o_ref[...] = _head_rms_gate(oprev_ref[...] + o_scr[...], og, hh_ref, ng_ref, GLA_DV)
    else:
        o_ref[...] = o_scr[...]


def _hgrn_kernel(*refs, reverse, final):
    if final:
        (z_ref, lb_ref, ebc_ref, msk_ref, tri_ref, blk_ref, sel_ref, oprev_ref, hh_ref, ng_ref,
         o_ref, st_scr, o_scr, qs, ks, vs, cs) = refs
    else:
        z_ref, lb_ref, ebc_ref, msk_ref, tri_ref, blk_ref, sel_ref, o_ref, st_scr, o_scr, qs, ks, vs, cs = refs

    @pl.when(pl.program_id(1) == 0)
    def _():
        st_scr[...] = jnp.zeros_like(st_scr)

    z = z_ref[...]
    lb = lb_ref[...]
    q = _silu(z[:, 0:HGRN_W])
    zf = z[:, (2 if reverse else 1) * HGRN_W:(3 if reverse else 2) * HGRN_W]
    v = z[:, 3 * HGRN_W:4 * HGRN_W]
    f = lb + (1.0 - lb) * jax.nn.sigmoid(zf)
    g = jnp.log(jnp.maximum(f, LOG_FLOOR))
    k = (1.0 - lb) * jax.nn.sigmoid(-zf)
    _gla_time_block(q, k, v, g, o_scr, st_scr, qs, ks, vs, cs, ebc_ref, msk_ref, tri_ref, blk_ref, sel_ref,
                    reverse)
    if final:
        og = z[:, 4 * HGRN_W:5 * HGRN_W]
        o_ref[...] = _head_rms_gate(oprev_ref[...] + o_scr[...], og, hh_ref, ng_ref, HGRN_DV)
    else:
        o_ref[...] = o_scr[...]


def _seq_row_map(nblk, reverse):
    if reverse:
        return lambda b, i: (b * nblk + nblk - 1 - i, 0)
    return lambda b, i: (b * nblk + i, 0)


def _const_map(b, i):
    return (0, 0)


def _lin_attn_pass(kernel, z, consts, o_prev, final_consts, bsz, t, w, reverse):
    n, zc = z.shape
    tb = min(GLA_TB, t)
    nblk = t // tb
    rmap = _seq_row_map(nblk, reverse)
    final = o_prev is not None
    r = jnp.arange(tb)
    same = (r[:, None] // GLA_CHUNK) == (r[None, :] // GLA_CHUNK)
    bound = [GLA_CHUNK * (r // GLA_CHUNK) + SUBLANES * b - (0 if reverse else 1)
             for b in range(1, GLA_CHUNK // SUBLANES)]
    sel = jnp.stack([(r[None, :] == rows[:, None]) for rows in bound]).astype(BF16)
    consts = tuple(consts) + ((same & (r[None, :] <= r[:, None])).astype(BF16), same.astype(BF16), sel)
    args = [z] + list(consts)
    in_specs = [pl.BlockSpec((tb, zc), rmap)]
    in_specs += [pl.BlockSpec(c.shape, lambda b, i, nd=c.ndim: (0,) * nd) for c in consts]
    if final:
        args += [o_prev] + list(final_consts)
        in_specs += [pl.BlockSpec((tb, MIX_W), rmap)]
        in_specs += [pl.BlockSpec(c.shape, _const_map) for c in final_consts]
    return pl.pallas_call(
        functools.partial(kernel, reverse=reverse, final=final),
        grid=(bsz, nblk),
        in_specs=in_specs,
        out_specs=pl.BlockSpec((tb, MIX_W), rmap),
        out_shape=jax.ShapeDtypeStruct((n, MIX_W), F32),
        scratch_shapes=[pltpu.VMEM((w // LANE, MIX_W * LANE // w, LANE), F32), pltpu.VMEM((tb, MIX_W), F32),
                        pltpu.VMEM((tb, w), F32), pltpu.VMEM((tb, w), F32),
                        pltpu.VMEM((tb, MIX_W), F32), pltpu.VMEM((tb, w), F32)],
        compiler_params=_params(("parallel", "arbitrary")),
    )(*args)


def _head_select(dk, heads):
    h = jnp.arange(heads)[:, None]
    lane = ((h * dk) // LANE) * LANE + jnp.arange(LANE)[None, :]
    own = (lane // dk == h).astype(BF16)
    return jnp.broadcast_to(own[:, :, None], (heads, LANE, LANE))


def _head_match(n_rows, row_blk, n_cols, col_blk, dtype):
    r = jnp.arange(n_rows) // row_blk
    c = jnp.arange(n_cols) // col_blk
    return (r[:, None] == c[None, :]).astype(dtype)


def _rwkv_prep_kernel(z_ref, zp_ref, zn_ref, muf_ref, mub_ref, w0f_ref, w2f_ref, w0b_ref, w2b_ref,
                      a0_ref, a2_ref, g2_ref, kk_ref, ka_ref, rk_ref, hh_ref,
                      r_out, kh_out, v_out, kkn_out, b_out, lwf_out, lwb_out, g_out, bonus_out):
    i = pl.program_id(1)
    last = pl.num_programs(1) - 1
    z = z_ref[...]
    tb = z.shape[0]
    rows = lax.broadcasted_iota(jnp.int32, (tb, 1), 0)
    hp = jnp.where(i == 0, 0.0, zp_ref[7:8, :])
    hn = jnp.where(i == last, 0.0, zn_ref[0:1, :])
    prev = jnp.where(rows == 0, hp, pltpu.roll(z, 1, 0))
    nxt = jnp.where(rows == tb - 1, hn, pltpu.roll(z, tb - 1, 0))
    p = z + muf_ref[...] * (prev - z) + mub_ref[...] * (nxt - z)
    r = p[:, 0:MIX_W]
    k = p[:, MIX_W:2 * MIX_W]
    v = p[:, 2 * MIX_W:3 * MIX_W]
    o = 3 * MIX_W
    wdf, wdb, ad, gd = (p[:, o + j * LANE:o + (j + 1) * LANE] for j in range(4))
    lwf = -RWKV_DECAY_SCALE * jax.nn.sigmoid(w0f_ref[...] + _dot(jnp.tanh(wdf), w2f_ref[...]))
    lwb = -RWKV_DECAY_SCALE * jax.nn.sigmoid(w0b_ref[...] + _dot(jnp.tanh(wdb), w2b_ref[...]))
    a = jax.nn.sigmoid(a0_ref[...] + _dot(ad, a2_ref[...]))
    g = _dot(jax.nn.sigmoid(gd), g2_ref[...])
    kk = k * kk_ref[...]
    ss = _dot_exact_rhs(kk * kk, hh_ref[...])
    kk = kk / jnp.maximum(jnp.sqrt(ss), 1e-12)
    kh = k * (1.0 + (a - 1.0) * ka_ref[...])
    bonus = _dot_exact_rhs(r * kh * rk_ref[...], hh_ref[...]) * v
    r_out[...] = r
    kh_out[...] = kh
    v_out[...] = v
    kkn_out[...] = kk
    b_out[...] = kk * a
    lwf_out[...] = lwf
    lwb_out[...] = lwb
    g_out[...] = g
    bonus_out[...] = bonus


def _rwkv_prep(z, consts, bsz, t):
    n, zc = z.shape
    tb = min(PREP_TB, t)
    nblk = t // tb
    hb = tb // 8
    nrow8 = n // 8
    rmap = _seq_row_map(nblk, False)
    pmap = lambda b, i: (jnp.maximum((b * nblk + i) * hb - 1, 0), 0)
    nmap = lambda b, i: (jnp.minimum((b * nblk + i + 1) * hb, nrow8 - 1), 0)
    in_specs = [pl.BlockSpec((tb, zc), rmap), pl.BlockSpec((8, zc), pmap), pl.BlockSpec((8, zc), nmap)]
    in_specs += [pl.BlockSpec(c.shape, _const_map) for c in consts]
    return pl.pallas_call(
        _rwkv_prep_kernel,
        grid=(bsz, nblk),
        in_specs=in_specs,
        out_specs=[pl.BlockSpec((tb, MIX_W), rmap)] * 9,
        out_shape=[jax.ShapeDtypeStruct((n, MIX_W), F32)] * 9,
        compiler_params=_params(("parallel", "parallel")),
    )(z, z, z, *consts)


def _rwkv_scan_kernel(r_ref, lw_ref, kh_ref, v_ref, kk_ref, b_ref, bd_ref, mstrict_ref, mincl_ref,
                      o_ref, s_scr, *, reverse):
    @pl.when(pl.program_id(1) == 0)
    def _():
        s_scr[...] = jnp.zeros_like(s_scr)

    L = RWKV_CHUNK
    tb = r_ref.shape[0]
    nchunks = tb // L
    tri = _tri_incl(L)
    lane = lax.broadcasted_iota(jnp.int32, (1, LANE), 1)
    m0 = (lane < RWKV_HEAD).astype(F32)
    m1 = 1.0 - m0
    eye = (lax.broadcasted_iota(jnp.int32, (2 * L, 2 * L), 0)
           == lax.broadcasted_iota(jnp.int32, (2 * L, 2 * L), 1)).astype(F32)
    mstrict = mstrict_ref[...]
    mincl = mincl_ref[...]
    groups = range(MIX_W // LANE)
    lanes = lambda p: slice(p * LANE, (p + 1) * LANE)
    stack2 = lambda x: jnp.concatenate([x * m0, x * m1], axis=0)

    pre = []
    for c in range(nchunks):
        sl = slice(c * L, (c + 1) * L)
        r, lw, kh, v, kk, b = (ref[sl, :] for ref in (r_ref, lw_ref, kh_ref, v_ref, kk_ref, b_ref))
        cum = _dot_exact_lhs(tri, lw)
        tot = cum[L - 1:L]
        inc = (tot - cum + lw) if reverse else cum
        exc = inc - lw
        mid = inc[L // 2:L // 2 + 1]
        e_pos = jnp.exp(inc - mid)
        e_neg = jnp.exp(mid - inc)
        e_end = jnp.exp(tot - inc)
        pre.append(dict(sl=sl, v=v, rt=r * e_pos, at=-kk * jnp.exp(exc - mid), bt=b * e_neg, kt=kh * e_neg,
                        a_abs=-kk * jnp.exp(exc), r_abs=r * jnp.exp(inc), dec=jnp.exp(tot),
                        wr=jnp.concatenate([b * e_end, kh * e_end], axis=0)))
    cells = [(c, p) for c in range(nchunks) for p in groups]
    gram = {}
    for c, p in cells:
        d = pre[c]
        lhs = jnp.concatenate([stack2(d['at'][:, lanes(p)]), stack2(d['rt'][:, lanes(p)])], axis=0)
        b_p, k_p = d['bt'][:, lanes(p)], d['kt'][:, lanes(p)]
        gram[c, p] = _dot_nt(lhs, jnp.concatenate([b_p, b_p, k_p, k_p], axis=0))
    n_ab = {k: g[0:2 * L, 0:2 * L] * mstrict for k, g in gram.items()}
    a_ak = {k: g[0:2 * L, 2 * L:4 * L] * mstrict for k, g in gram.items()}
    a_rb = {k: g[2 * L:4 * L, 0:2 * L] * mincl for k, g in gram.items()}
    a_rk = {k: g[2 * L:4 * L, 2 * L:4 * L] * mincl for k, g in gram.items()}
    v_bd = {(c, p): stack2(pre[c]['v'][:, lanes(p)]) for c, p in cells}
    x0 = {k: _dot(a_ak[k], v_bd[k]) for k in cells}
    o0 = {k: _dot(a_rk[k], v_bd[k]) for k in cells}
    tinv = {k: eye + n_ab[k] for k in cells}
    pw = dict(n_ab)
    for _ in range(5):
        pw = {k: _dot(pw[k], pw[k]) for k in cells}
        tinv = {k: tinv[k] + _dot(pw[k], tinv[k]) for k in cells}

    fold2 = lambda x: x[0:L] + x[L:2 * L]
    for c in (range(nchunks - 1, -1, -1) if reverse else range(nchunks)):
        d = pre[c]
        s = s_scr[...]
        am = _dot_nt(d['a_abs'], s)
        rm = _dot_nt(d['r_abs'], s)
        us = [_dot(tinv[c, p], stack2(am[:, lanes(p)]) + x0[c, p]) for p in groups]
        o_bd = [_dot(a_rb[c, p], us[p]) + o0[c, p] for p in groups]
        o_ref[d['sl'], :] = rm + jnp.concatenate([fold2(o) for o in o_bd], axis=1)
        u_all = jnp.concatenate([fold2(u) for u in us], axis=1)
        upd = _dot_tn(jnp.concatenate([u_all, d['v']], axis=0), d['wr'])
        s_scr[...] = s * d['dec'] + upd * bd_ref[...]


def _rwkv_scan(r, lw, kh, v, kk, b, consts, bsz, t, reverse):
    n = r.shape[0]
    tb = min(RWKV_TB, t)
    nblk = t // tb
    rmap = _seq_row_map(nblk, reverse)
    row_spec = pl.BlockSpec((tb, MIX_W), rmap)
    return pl.pallas_call(
        functools.partial(_rwkv_scan_kernel, reverse=reverse),
        grid=(bsz, nblk),
        in_specs=[row_spec] * 6 + [pl.BlockSpec(c.shape, _const_map) for c in consts],
        out_specs=row_spec,
        out_shape=jax.ShapeDtypeStruct((n, MIX_W), F32),
        scratch_shapes=[pltpu.VMEM((MIX_W, MIX_W), F32)],
        compiler_params=_params(("parallel", "arbitrary")),
    )(r, lw, kh, v, kk, b, *consts)


def _merge_kernel(x_ref, ogla_ref, ohg_ref, orf_ref, orb_ref, bonus_ref, g_ref, zg_ref,
                  wg_ref, wh_ref, wr_ref, wo_ref, ng_ref, nb_ref, hh_ref, o_ref):
    o = orf_ref[...] + orb_ref[...]
    inv = 1.0 / RWKV_HEAD
    mean = _dot_exact_rhs(o, hh_ref[...]) * inv
    d = o - mean
    var = _dot_exact_rhs(d * d, hh_ref[...]) * inv
    on = d * lax.rsqrt(var + RWKV_GN_EPS) * ng_ref[...] + nb_ref[...]
    orw = (on + bonus_ref[...]) * g_ref[...]
    zg = zg_ref[...]
    gate = lambda j: jax.nn.sigmoid(zg[:, j * D_MODEL:(j + 1) * D_MODEL])
    merged = (gate(0) * _dot(ogla_ref[...], wg_ref[...])
              + gate(1) * _dot(ohg_ref[...], wh_ref[...])
              + gate(2) * _dot(orw, wr_ref[...]))
    o_ref[...] = x_ref[...] + _dot(merged, wo_ref[...])


def _merge(x, rows, consts):
    n, d = x.shape
    tm = min(MERGE_TILE, n)
    rspec = lambda a: pl.BlockSpec((tm, a.shape[1]), lambda i: (i, 0))
    return pl.pallas_call(
        _merge_kernel,
        grid=(n // tm,),
        in_specs=[rspec(x)] + [rspec(a) for a in rows]
        + [pl.BlockSpec(c.shape, lambda i: (0, 0)) for c in consts],
        out_specs=pl.BlockSpec((tm, d), lambda i: (i, 0)),
        out_shape=jax.ShapeDtypeStruct((n, d), F32),
        compiler_params=_params(("parallel",)),
    )(x, *rows, *consts)


def _xattn_kernel(x_ref, kv_ref, g_ref, wq_ref, wo_ref, o_ref):
    x = x_ref[...]
    h = _rmsnorm(x, g_ref[...])
    q = _dot(h, wq_ref[...])
    kv = kv_ref[...]
    outs = []
    for hd in range(X_HEADS):
        ls = slice(hd * X_HEAD, (hd + 1) * X_HEAD)
        s = _dot_nt(q[:, ls], kv[:, ls]) * (X_HEAD ** -0.5)
        s = s - jnp.max(s, axis=-1, keepdims=True)
        e = jnp.exp(s)
        pr = e / jnp.sum(e, axis=-1, keepdims=True)
        outs.append(_dot(pr, kv[:, D_MODEL + hd * X_HEAD:D_MODEL + (hd + 1) * X_HEAD]))
    o_ref[...] = x + _dot(jnp.concatenate(outs, axis=1), wo_ref[...])


def _xattn(x, kv, g, wq, wo, bsz, t, n_mem):
    n, d = x.shape
    tq = min(XATTN_TQ, t)
    nblk = t // tq
    return pl.pallas_call(
        _xattn_kernel,
        grid=(bsz, nblk),
        in_specs=[pl.BlockSpec((tq, d), lambda b, i: (b * nblk + i, 0)),
                  pl.BlockSpec((n_mem, 2 * d), lambda b, i: (b, 0)),
                  pl.BlockSpec((1, d), _const_map),
                  pl.BlockSpec((d, d), _const_map),
                  pl.BlockSpec((d, d), _const_map)],
        out_specs=pl.BlockSpec((tq, d), lambda b, i: (b * nblk + i, 0)),
        out_shape=jax.ShapeDtypeStruct((n, d), F32),
        compiler_params=_params(("parallel", "parallel")),
    )(x, kv, g.reshape(1, d), wq, wo)


def _top16_rows(x):
    nrow = x.shape[0]
    rows = lax.broadcasted_iota(jnp.int32, x.shape, 0)
    vals = []
    for _ in range(PEER_TOPK):
        m = jnp.max(x, axis=0, keepdims=True)
        pos = jnp.min(jnp.where(x == m, rows, nrow), axis=0, keepdims=True)
        vals.append(m)
        x = jnp.where(rows == pos, -jnp.inf, x)
    return jnp.concatenate(vals, 0)


def _peer_topk_kernel(q_ref, sk1_ref, sk2_ref, s1_ref, s2_ref, st_ref):
    q = q_ref[...]
    thr, mx1, mx2, rz = [], [], [], []
    for hd in range(PEER_HEADS):
        hs = slice(hd * PEER_NKEYS, (hd + 1) * PEER_NKEYS)
        qh = q[:, hd * PEER_DK:(hd + 1) * PEER_DK]
        s1 = _dot_nt_f32(sk1_ref[...], qh)
        s2 = _dot_nt_f32(sk2_ref[...], qh)
        s1_ref[hs, :] = s1
        s2_ref[hs, :] = s2
        v1 = _top16_rows(s1)
        v2 = _top16_rows(s2)
        half = PEER_TOPK // 2
        cand = jnp.concatenate([v1[0:1] + v2] + [v1[a:a + 1] + v2[0:half] for a in range(1, half)]
                               + [v1[half:] + v2[0:1]], axis=0)
        top = _top16_rows(cand)
        thr.append(top[PEER_TOPK - 1:PEER_TOPK])
        mx1.append(v1[0:1])
        mx2.append(v2[0:1])
        rz.append(1.0 / jnp.sum(jnp.exp(top - top[0:1]), axis=0, keepdims=True))
    st_ref[...] = jnp.concatenate(thr + mx1 + mx2 + rz, axis=0)


def _peer_topk(q, sk1p, sk2p):
    n, d = q.shape
    tb = min(TOPK_TB, n)
    nrow = PEER_HEADS * PEER_NKEYS
    col = lambda r: pl.BlockSpec((r, tb), lambda i: (0, i))
    return pl.pallas_call(
        _peer_topk_kernel,
        grid=(n // tb,),
        in_specs=[pl.BlockSpec((tb, d), lambda i: (i, 0)),
                  pl.BlockSpec(sk1p.shape, lambda i: (0, 0)),
                  pl.BlockSpec(sk2p.shape, lambda i: (0, 0))],
        out_specs=[col(nrow), col(nrow), col(4 * PEER_HEADS)],
        out_shape=[jax.ShapeDtypeStruct((nrow, n), F32), jax.ShapeDtypeStruct((nrow, n), F32),
                   jax.ShapeDtypeStruct((4 * PEER_HEADS, n), F32)],
        compiler_params=_params(("parallel",)),
    )(q, sk1p, sk2p)


def _peer_expert_kernel(h_ref, u_ref, vt_ref, s1_ref, s2_ref, st_ref, x_ref, o_ref,
                        acc_scr, e1_scr, e2_scr):
    j = pl.program_id(1)
    nh, nk = PEER_HEADS, PEER_NKEYS

    @pl.when(j == 0)
    def _():
        acc_scr[...] = jnp.zeros_like(acc_scr)
        for hd in range(nh):
            hs = slice(hd * nk, (hd + 1) * nk)
            e1_scr[hs, :] = jnp.exp(s1_ref[hs, :] - st_ref[nh + hd:nh + hd + 1, :])
            e2_scr[hs, :] = (jnp.exp(s2_ref[hs, :] - st_ref[2 * nh + hd:2 * nh + hd + 1, :])
                             * st_ref[3 * nh + hd:3 * nh + hd + 1, :])

    et, tb = u_ref.shape[0], h_ref.shape[0]
    sub = PEER_MM // nk
    for sl in range(et // PEER_MM):
        rows = slice(sl * PEER_MM, (sl + 1) * PEER_MM)
        act = _dot_nt(u_ref[rows, :], h_ref[...])
        coef = []
        for il in range(sub):
            i1 = (j * (et // PEER_MM) + sl) * sub + il
            parts = []
            s1_rows = [s1_ref[pl.ds(hd * nk + i1, 1), :] for hd in range(nh)]
            e1_rows = [e1_scr[pl.ds(hd * nk + i1, 1), :] for hd in range(nh)]
            for lt in range(tb // LANE):
                ts = slice(lt * LANE, (lt + 1) * LANE)
                gate = jnp.zeros((nk, LANE), F32)
                for hd in range(nh):
                    hs = slice(hd * nk, (hd + 1) * nk)
                    score = s1_rows[hd][:, ts] + s2_ref[hs, ts]
                    weight = e1_rows[hd][:, ts] * e2_scr[hs, ts]
                    gate = gate + jnp.where(score >= st_ref[hd:hd + 1, ts], weight, 0.0)
                a = act[il * nk:(il + 1) * nk, ts]
                gelu = 0.5 * a * (1.0 + lax.erf(a * (2.0 ** -0.5)))
                parts.append((gate * gelu).astype(BF16))
            coef.append(jnp.concatenate(parts, axis=1))
        acc_scr[...] += jnp.dot(vt_ref[:, rows], jnp.concatenate(coef, axis=0),
                                preferred_element_type=F32)

    @pl.when(j == pl.num_programs(1) - 1)
    def _():
        o_ref[...] = x_ref[...] + acc_scr[...].T


def _peer_experts(h, s1, s2, st, x, u, vt):
    n, d = x.shape
    ne = u.shape[0]
    tb = min(PEER_TB, n)
    et = PEER_ET
    nrow = s1.shape[0]
    col = lambda r: pl.BlockSpec((r, tb), lambda i, j: (0, i))
    row = pl.BlockSpec((tb, d), lambda i, j: (i, 0))
    return pl.pallas_call(
        _peer_expert_kernel,
        grid=(n // tb, ne // et),
        in_specs=[row, pl.BlockSpec((et, d), lambda i, j: (j, 0)),
                  pl.BlockSpec((d, et), lambda i, j: (0, j)),
                  col(nrow), col(nrow), col(st.shape[0]), row],
        out_specs=row,
        out_shape=jax.ShapeDtypeStruct((n, d), F32),
        scratch_shapes=[pltpu.VMEM((d, tb), F32), pltpu.VMEM((nrow, tb), F32),
                        pltpu.VMEM((nrow, tb), F32)],
        compiler_params=_params(("parallel", "arbitrary")),
    )(h, u, vt, s1, s2, st, x)


def _pad_cols(a, width):
    return jnp.pad(a, ((0, 0), (0, width - a.shape[1])))


def _pad_rows(a, height):
    return jnp.pad(a, ((0, height - a.shape[0]), (0, 0)))


def _pack_rwkv_cols(a):
    o = 3 * MIX_W
    wf = a[:, o:o + RWKV_DECAY_RANK]
    wb = a[:, o + RWKV_DECAY_RANK:o + 2 * RWKV_DECAY_RANK]
    ad = a[:, o + 2 * RWKV_DECAY_RANK:o + 2 * RWKV_DECAY_RANK + RWKV_AAA_RANK]
    gd = a[:, o + 2 * RWKV_DECAY_RANK + RWKV_AAA_RANK:]
    return jnp.concatenate([a[:, :o], _pad_cols(wf, LANE), _pad_cols(wb, LANE), _pad_cols(ad, LANE),
                            _pad_cols(gd, LANE)], axis=1)


def _layer_weights(P, l, lb):
    w_in = P['w_in'][l]
    row = lambda a: a.reshape(1, -1).astype(F32)
    W = {}
    W['norm_mix_g'] = P['norm_mix_g'][l]
    W['w_gla'] = _pad_cols(w_in[:, :GLA_IN], ZGLA_W).astype(BF16)
    W['w_hgrn'] = w_in[:, GLA_IN:GLA_IN + HGRN_IN].astype(BF16)
    W['w_rwkv'] = _pack_rwkv_cols(w_in[:, GLA_IN + HGRN_IN:GATE_OFF]).astype(BF16)
    W['w_gate'] = w_in[:, GATE_OFF:].astype(BF16)
    W['gla_up_f'] = _pad_rows(P['gla_gate_up_f'][l], LANE).astype(BF16)
    W['gla_up_b'] = _pad_rows(jnp.concatenate(
        [jnp.zeros_like(P['gla_gate_up_b'][l]), P['gla_gate_up_b'][l]], axis=0), LANE).astype(BF16)
    W['gla_bias_f'] = row(P['gla_gate_bias_f'][l])
    W['gla_bias_b'] = row(P['gla_gate_bias_b'][l])
    W['gla_norm_g'] = row(P['gla_norm_g'][l])
    W['hgrn_lb'] = row(lb)
    W['hgrn_norm_g'] = row(P['hgrn_norm_g'][l])
    W['rwkv_mu_f'] = _pack_rwkv_cols(row(P['rwkv_mu_f'][l]))
    W['rwkv_mu_b'] = _pack_rwkv_cols(row(P['rwkv_mu_b'][l]))
    W['rwkv_w0_f'] = row(P['rwkv_w0_f'][l])
    W['rwkv_w2_f'] = _pad_rows(P['rwkv_w2_f'][l], LANE).astype(BF16)
    W['rwkv_w0_b'] = row(P['rwkv_w0_b'][l])
    W['rwkv_w2_b'] = _pad_rows(P['rwkv_w2_b'][l], LANE).astype(BF16)
    W['rwkv_a0'] = row(P['rwkv_a0'][l])
    W['rwkv_a2'] = _pad_rows(P['rwkv_a2'][l], LANE).astype(BF16)
    W['rwkv_g2'] = P['rwkv_g2'][l].astype(BF16)
    for name in ('rwkv_k_k', 'rwkv_k_a', 'rwkv_r_k', 'rwkv_norm_g', 'rwkv_norm_b'):
        W[name] = row(P[name][l])
    for name in ('w_branch_gla', 'w_branch_hgrn', 'w_branch_rwkv', 'w_out', 'xattn_wq', 'xattn_wo'):
        W[name] = P[name][l].astype(BF16)
    W['peer_wq'] = P['peer_wq'][l]
    W['xattn_wkv'] = jnp.concatenate([P['xattn_wk'][l], P['xattn_wv'][l]], axis=1).astype(BF16)
    for name in ('norm_x_g', 'norm_mem_g', 'norm_ffn_g'):
        W[name] = P[name][l]
    half = PEER_DK // 2
    W['peer_sk1'] = jnp.pad(P['peer_subkeys_1'][l], ((0, 0), (0, half)))
    W['peer_sk2'] = jnp.pad(P['peer_subkeys_2'][l], ((0, 0), (half, 0)))
    W['peer_u'] = P['peer_u'][l].astype(BF16)
    W['peer_vt'] = P['peer_v'][l].astype(BF16).T
    return W


def _shared_consts():
    L = RWKV_CHUNK
    idx = jnp.arange(2 * L)
    same = (idx[:, None] // L) == (idx[None, :] // L)
    tpos = idx % L
    C = {
        'gla_ebc': _head_select(GLA_DK, GLA_HEADS),
        'gla_bdt': _head_match(MIX_W * LANE // GLA_W, GLA_DV, LANE, GLA_DK, F32),
        'hgrn_ebc': _head_select(HGRN_DK, HGRN_HEADS),
        'hgrn_bdt': _head_match(MIX_W * LANE // HGRN_W, HGRN_DV, LANE, HGRN_DK, F32),
        'hh128': _head_match(MIX_W, 128, MIX_W, 128, BF16),
        'hh64': _head_match(MIX_W, RWKV_HEAD, MIX_W, RWKV_HEAD, BF16),
        'rwkv_bd': _head_match(MIX_W, RWKV_HEAD, MIX_W, RWKV_HEAD, F32),
        'strict_f': (same & (tpos[None, :] < tpos[:, None])).astype(F32),
        'incl_f': (same & (tpos[None, :] <= tpos[:, None])).astype(F32),
        'strict_b': (same & (tpos[None, :] > tpos[:, None])).astype(F32),
        'incl_b': (same & (tpos[None, :] >= tpos[:, None])).astype(F32),
    }
    return C


def _encoder_layer(x, mem, W, C, bsz, t, n_mem):
    g_mix = W['norm_mix_g']
    z_gla = _norm_matmul(x, g_mix, W['w_gla'])
    z_hgrn = _norm_matmul(x, g_mix, W['w_hgrn'])
    z_rwkv = _norm_matmul(x, g_mix, W['w_rwkv'])
    z_gate = _norm_matmul(x, g_mix, W['w_gate'])

    gla_c = (C['gla_ebc'], C['gla_bdt'])
    o_f = _lin_attn_pass(_gla_kernel, z_gla, (W['gla_up_f'], W['gla_bias_f']) + gla_c, None, None,
                         bsz, t, GLA_W, False)
    o_gla = _lin_attn_pass(_gla_kernel, z_gla, (W['gla_up_b'], W['gla_bias_b']) + gla_c, o_f,
                           (C['hh128'], W['gla_norm_g']), bsz, t, GLA_W, True)

    hg_c = (W['hgrn_lb'], C['hgrn_ebc'], C['hgrn_bdt'])
    o_f = _lin_attn_pass(_hgrn_kernel, z_hgrn, hg_c, None, None, bsz, t, HGRN_W, False)
    o_hgrn = _lin_attn_pass(_hgrn_kernel, z_hgrn, hg_c, o_f, (C['hh128'], W['hgrn_norm_g']),
                            bsz, t, HGRN_W, True)

    prep_c = (W['rwkv_mu_f'], W['rwkv_mu_b'], W['rwkv_w0_f'], W['rwkv_w2_f'], W['rwkv_w0_b'],
              W['rwkv_w2_b'], W['rwkv_a0'], W['rwkv_a2'], W['rwkv_g2'], W['rwkv_k_k'], W['rwkv_k_a'],
              W['rwkv_r_k'], C['hh64'])
    r, kh, v, kk, b, lw_f, lw_b, g, bonus = _rwkv_prep(z_rwkv, prep_c, bsz, t)
    o_rf = _rwkv_scan(r, lw_f, kh, v, kk, b, (C['rwkv_bd'], C['strict_f'], C['incl_f']), bsz, t, False)
    o_rb = _rwkv_scan(r, lw_b, kh, v, kk, b, (C['rwkv_bd'], C['strict_b'], C['incl_b']), bsz, t, True)

    x = _merge(x, (o_gla, o_hgrn, o_rf, o_rb, bonus, g, z_gate),
               (W['w_branch_gla'], W['w_branch_hgrn'], W['w_branch_rwkv'], W['w_out'],
                W['rwkv_norm_g'], W['rwkv_norm_b'], C['hh64']))

    kv = _norm_matmul(mem, W['norm_mem_g'], W['xattn_wkv'])
    x = _xattn(x, kv, W['norm_x_g'], W['xattn_wq'], W['xattn_wo'], bsz, t, n_mem)

    q, h = _norm_matmul(x, W['norm_ffn_g'], W['peer_wq'], emit_h=True)
    s1, s2, st = _peer_topk(q, W['peer_sk1'], W['peer_sk2'])
    return _peer_experts(h, s1, s2, st, x, W['peer_u'], W['peer_vt'])


def _run_trunk(x, mem, P, weights, C):
    bsz, t, d = x.shape
    n_mem = mem.shape[1]
    x = x.reshape(bsz * t, d)
    mem = mem.reshape(bsz * n_mem, d)
    for W in weights:
        x = _encoder_layer(x, mem, W, C, bsz, t, n_mem)
    return _final_norm(x, P['final_norm_g']).reshape(bsz, t, d)


def _hgrn_lower_bounds(logits):
    sm = jax.nn.softmax(logits.astype(F32), axis=0)
    return jnp.cumsum(sm, axis=0) - sm[0]


def kernel(x_prompt, x_sample, mem_prompt, mem_sample, norm_mix_g, w_in, gla_gate_up_f, gla_gate_up_b, gla_gate_bias_f, gla_gate_bias_b, gla_norm_g, hgrn_lb_logits, hgrn_norm_g, rwkv_mu_f, rwkv_mu_b, rwkv_w0_f, rwkv_w2_f, rwkv_w0_b, rwkv_w2_b, rwkv_a0, rwkv_a2, rwkv_g2, rwkv_k_k, rwkv_k_a, rwkv_r_k, rwkv_norm_g, rwkv_norm_b, w_branch_gla, w_branch_hgrn, w_branch_rwkv, w_out, norm_x_g, norm_mem_g, xattn_wq, xattn_wk, xattn_wv, xattn_wo, norm_ffn_g, peer_wq, peer_subkeys_1, peer_subkeys_2, peer_u, peer_v, final_norm_g):
    P = dict(norm_mix_g=norm_mix_g, w_in=w_in, gla_gate_up_f=gla_gate_up_f, gla_gate_up_b=gla_gate_up_b,
             gla_gate_bias_f=gla_gate_bias_f, gla_gate_bias_b=gla_gate_bias_b, gla_norm_g=gla_norm_g,
             hgrn_lb_logits=hgrn_lb_logits, hgrn_norm_g=hgrn_norm_g, rwkv_mu_f=rwkv_mu_f, rwkv_mu_b=rwkv_mu_b,
             rwkv_w0_f=rwkv_w0_f, rwkv_w2_f=rwkv_w2_f, rwkv_w0_b=rwkv_w0_b, rwkv_w2_b=rwkv_w2_b,
             rwkv_a0=rwkv_a0, rwkv_a2=rwkv_a2, rwkv_g2=rwkv_g2, rwkv_k_k=rwkv_k_k, rwkv_k_a=rwkv_k_a,
             rwkv_r_k=rwkv_r_k, rwkv_norm_g=rwkv_norm_g, rwkv_norm_b=rwkv_norm_b, w_branch_gla=w_branch_gla,
             w_branch_hgrn=w_branch_hgrn, w_branch_rwkv=w_branch_rwkv, w_out=w_out, norm_x_g=norm_x_g,
             norm_mem_g=norm_mem_g, xattn_wq=xattn_wq, xattn_wk=xattn_wk, xattn_wv=xattn_wv, xattn_wo=xattn_wo,
             norm_ffn_g=norm_ffn_g, peer_wq=peer_wq, peer_subkeys_1=peer_subkeys_1,
             peer_subkeys_2=peer_subkeys_2, peer_u=peer_u, peer_v=peer_v, final_norm_g=final_norm_g)
    depth = w_in.shape[0]
    lbs = _hgrn_lower_bounds(hgrn_lb_logits)
    weights = [_layer_weights(P, l, lbs[l]) for l in range(depth)]
    C = _shared_consts()
    return (_run_trunk(x_prompt, mem_prompt, P, weights, C),
            _run_trunk(x_sample, mem_sample, P, weights, C))
```

```python
import functools

import jax
import jax.numpy as jnp
from jax import lax
from jax.experimental import pallas as pl
from jax.experimental.pallas import tpu as pltpu

F32 = jnp.float32
BF16 = jnp.bfloat16

D_MODEL = 1024
EPS = 1e-6
LOG_FLOOR = 1e-30
DEAD_EXPONENT = -1e30
MIX_W = 512
GLA_HEADS, GLA_DK, GLA_DV = 4, 64, 128
GLA_GATE_RANK = 16
GLA_GATE_NORM = 16.0
HGRN_HEADS, HGRN_DK, HGRN_DV = 4, 128, 128
RWKV_HEADS, RWKV_HEAD = 8, 64
RWKV_DECAY_RANK, RWKV_AAA_RANK, RWKV_GATE_RANK = 64, 64, 128
RWKV_DECAY_SCALE = 0.606531
RWKV_GN_EPS = 64e-5
X_HEADS = 4
X_HEAD = D_MODEL // X_HEADS
PEER_HEADS, PEER_DK, PEER_NKEYS, PEER_TOPK = 8, 128, 128, 16

GLA_W = GLA_HEADS * GLA_DK
GLA_IN = 2 * GLA_W + 2 * MIX_W + 2 * GLA_GATE_RANK
HGRN_W = HGRN_HEADS * HGRN_DK
HGRN_IN = 5 * MIX_W
RWKV_IN = 3 * MIX_W + 2 * RWKV_DECAY_RANK + RWKV_AAA_RANK + RWKV_GATE_RANK
GATE_OFF = GLA_IN + HGRN_IN + RWKV_IN

LANE = 128
SUBLANES = 8
ZGLA_W = 2 * GLA_W + 2 * MIX_W + LANE
ZRWKV_W = 3 * MIX_W + 4 * LANE

ROW_TILE = 512
MERGE_TILE = 256
GLA_CHUNK = 32
GLA_TB = 256
RWKV_CHUNK = 64
RWKV_TB = 256
PREP_TB = 256
XATTN_TQ = 256
TOPK_TB = 128
PEER_TB = 256
PEER_ET = 2048
PEER_MM = 1024
PEER_MM1 = 256
VMEM_LIMIT = 48 * 1024 * 1024


def _params(sem):
    return pltpu.CompilerParams(dimension_semantics=sem, vmem_limit_bytes=VMEM_LIMIT)


def _dot(a, b):
    return jnp.dot(a.astype(BF16), b.astype(BF16), preferred_element_type=F32)


def _dot_nt(a, b):
    return lax.dot_general(a.astype(BF16), b.astype(BF16), (((1,), (1,)), ((), ())),
                           preferred_element_type=F32)


def _dot_nt_f32(a, b):
    return lax.dot_general(a, b, (((1,), (1,)), ((), ())), precision=lax.Precision.HIGHEST,
                           preferred_element_type=F32)


def _dot_tn(a, b):
    return lax.dot_general(a.astype(BF16), b.astype(BF16), (((0,), (0,)), ((), ())),
                           preferred_element_type=F32)


def _split3(x):
    hi = x.astype(BF16)
    r1 = x - hi.astype(F32)
    mid = r1.astype(BF16)
    lo = (r1 - mid.astype(F32)).astype(BF16)
    return hi, mid, lo


def _dot_exact_rhs(x, m):
    hi, mid, lo = _split3(x)
    f = lambda p: jnp.dot(p, m, preferred_element_type=F32)
    return f(hi) + f(mid) + f(lo)


def _dot_exact_lhs(m, x):
    hi, mid, lo = _split3(x)
    f = lambda p: jnp.dot(m, p, preferred_element_type=F32)
    return f(hi) + f(mid) + f(lo)


def _tri_incl(n):
    r = lax.broadcasted_iota(jnp.int32, (n, n), 0)
    c = lax.broadcasted_iota(jnp.int32, (n, n), 1)
    return (r >= c).astype(BF16)


def _rmsnorm(x, g):
    return x * lax.rsqrt(jnp.mean(x * x, axis=-1, keepdims=True) + EPS) * g


def _log_sigmoid(x):
    return jnp.minimum(x, 0.0) - jnp.log(1.0 + jnp.exp(-jnp.abs(x)))


def _silu(x):
    return x * jax.nn.sigmoid(x)


def _norm_matmul_kernel(x_ref, g_ref, w_ref, o_ref, *h_ref):
    h = _rmsnorm(x_ref[...], g_ref[...])
    if w_ref.dtype == F32:
        o_ref[...] = jnp.dot(h, w_ref[...], precision=lax.Precision.HIGHEST, preferred_element_type=F32)
    else:
        o_ref[...] = jnp.dot(h.astype(BF16), w_ref[...], preferred_element_type=F32)
    if h_ref:
        h_ref[0][...] = h.astype(h_ref[0].dtype)


def _norm_matmul(x, g, w, emit_h=False):
    n, d = x.shape
    c = w.shape[1]
    tm = min(ROW_TILE, n)
    out_shape = [jax.ShapeDtypeStruct((n, c), F32)]
    out_specs = [pl.BlockSpec((tm, c), lambda i: (i, 0))]
    if emit_h:
        out_shape.append(jax.ShapeDtypeStruct((n, d), BF16))
        out_specs.append(pl.BlockSpec((tm, d), lambda i: (i, 0)))
    res = pl.pallas_call(
        _norm_matmul_kernel,
        grid=(n // tm,),
        in_specs=[pl.BlockSpec((tm, d), lambda i: (i, 0)),
                  pl.BlockSpec((1, d), lambda i: (0, 0)),
                  pl.BlockSpec((d, c), lambda i: (0, 0))],
        out_specs=out_specs,
        out_shape=out_shape,
        compiler_params=_params(("parallel",)),
    )(x, g.reshape(1, d), w)
    return res if emit_h else res[0]


def _final_norm_kernel(x_ref, g_ref, o_ref):
    o_ref[...] = _rmsnorm(x_ref[...], g_ref[...])


def _final_norm(x, g):
    n, d = x.shape
    tm = min(ROW_TILE, n)
    return pl.pallas_call(
        _final_norm_kernel,
        grid=(n // tm,),
        in_specs=[pl.BlockSpec((tm, d), lambda i: (i, 0)), pl.BlockSpec((1, d), lambda i: (0, 0))],
        out_specs=pl.BlockSpec((tm, d), lambda i: (i, 0)),
        out_shape=jax.ShapeDtypeStruct((n, d), F32),
        compiler_params=_params(("parallel",)),
    )(x, g.reshape(1, d))


def _gla_time_block(q, k, v, g, o_scr, st_scr, qs, ks, vs, cs, ebc_ref, msk_ref, tri_ref, blk_ref, sel_ref,
                    reverse):
    tb, w = q.shape
    hv = v.shape[1]
    C, S = GLA_CHUNK, SUBLANES
    nchunks, nsub = tb // C, C // S
    cum = _dot_exact_lhs(tri_ref[...], g)
    tot = _dot_exact_lhs(blk_ref[...], g)
    if reverse:
        cum = tot - cum + g
    qs[...] = q
    ks[...] = k
    vs[...] = v
    cs[...] = cum
    sub = (lax.broadcasted_iota(jnp.int32, (tb, 1), 0) % C) // S
    qf, kf = [], []
    for bd in range(1, nsub):
        ref = _dot_exact_lhs(sel_ref[bd - 1], cum)
        if reverse:
            q_rows, k_rows = sub == bd - 1, sub >= bd
        else:
            q_rows, k_rows = sub == bd, sub < bd
        qf.append(q * jnp.exp(jnp.where(q_rows, cum - ref, DEAD_EXPONENT)))
        kf.append(k * jnp.exp(jnp.where(k_rows, ref - cum, DEAD_EXPONENT)))
    same_chunk = blk_ref[...].astype(F32)
    pos = lax.broadcasted_iota(jnp.int32, (S, 1), 0)
    lane = lax.broadcasted_iota(jnp.int32, (1, LANE), 1)
    heads = hv // LANE
    dk = w // heads
    for h in range(heads):
        lt = (h * dk) // LANE
        ls = slice(lt * LANE, (lt + 1) * LANE)
        hs = slice(h * LANE, (h + 1) * LANE)
        o = jnp.zeros((tb, LANE), F32)
        for b in range(S):
            live = (pos <= b) if reverse else (pos >= b)
            w_ij, v_j = [], []
            for sb in range(tb // S):
                rs = slice(sb * S, (sb + 1) * S)
                row = slice(sb * S + b, sb * S + b + 1)
                v_j.append(vs[row, hs])
                decay = jnp.exp(jnp.where(live, cs[rs, ls] - cs[row, ls], DEAD_EXPONENT))
                w_ij.append((qs[rs, ls] * ks[row, ls] * decay).astype(BF16))
            s_j = jnp.dot(jnp.concatenate(w_ij, axis=0), ebc_ref[h], preferred_element_type=F32)
            o = o + jnp.concatenate([s_j[sb * S:(sb + 1) * S] * v_j[sb] for sb in range(tb // S)], axis=0)
        own = ((lane // dk) == (h % (LANE // dk))).astype(F32) if dk < LANE else None
        scores = jnp.zeros((tb, tb), F32)
        for bd in range(nsub - 1):
            q_h = qf[bd][:, ls] if own is None else qf[bd][:, ls] * own
            scores = scores + _dot_nt(q_h, kf[bd][:, ls])
        o_scr[:, hs] = o + _dot(scores * same_chunk, v[:, hs])
    qe = q * jnp.exp(cum)
    kd = k * jnp.exp(tot - cum)
    dec = jnp.exp(tot)
    ntile = w // LANE
    vw = hv // ntile
    for c in (range(nchunks - 1, -1, -1) if reverse else range(nchunks)):
        sl = slice(c * C, (c + 1) * C)
        st = [st_scr[p] for p in range(ntile)]
        inter = [_dot_nt(qe[sl, p * LANE:(p + 1) * LANE], st[p]) for p in range(ntile)]
        o_scr[sl, :] = o_scr[sl, :] + jnp.concatenate(inter, axis=1)
        upd = [_dot_tn(v[sl, p * vw:(p + 1) * vw], kd[sl, p * LANE:(p + 1) * LANE]) for p in range(ntile)]
        for p in range(ntile):
            if vw > LANE:
                upd[p] = upd[p] * msk_ref[...]
            st_scr[p] = st[p] * dec[c * C:c * C + 1, p * LANE:(p + 1) * LANE] + upd[p]


def _head_rms_gate(o, og, hh_ref, ng_ref, dv):
    ms = _dot_exact_rhs(o * o, hh_ref[...]) * (1.0 / dv)
    return o * lax.rsqrt(ms + EPS) * ng_ref[...] * _silu(og)


def _gla_kernel(*refs, reverse, final):
    if final:
        (z_ref, up_ref, bias_ref, ebc_ref, msk_ref, tri_ref, blk_ref, sel_ref, oprev_ref, hh_ref, ng_ref,
         o_ref, st_scr, o_scr, qs, ks, vs, cs) = refs
    else:
        (z_ref, up_ref, bias_ref, ebc_ref, msk_ref, tri_ref, blk_ref, sel_ref,
         o_ref, st_scr, o_scr, qs, ks, vs, cs) = refs

    @pl.when(pl.program_id(1) == 0)
    def _():
        st_scr[...] = jnp.zeros_like(st_scr)

    z = z_ref[...]
    q = z[:, 0:GLA_W] * (GLA_DK ** -0.5)
    k = z[:, GLA_W:2 * GLA_W]
    v = z[:, 2 * GLA_W:2 * GLA_W + MIX_W]
    gd = z[:, 2 * GLA_W + 2 * MIX_W:ZGLA_W]
    g = _log_sigmoid(_dot(gd, up_ref[...]) + bias_ref[...]) * (1.0 / GLA_GATE_NORM)
    _gla_time_block(q, k, v, g, o_scr, st_scr, qs, ks, vs, cs, ebc_ref, msk_ref, tri_ref, blk_ref, sel_ref,
                    reverse)
    if final:
        og = z[:, 2 * GLA_W + MIX_W:2 * GLA_W + 2 * MIX_W]
        o_ref[...] = _head_rms_gate(oprev_ref[...] + o_scr[...], og, hh_ref, ng_ref, GLA_DV)
    else:
        o_ref[...] = o_scr[...]


def _hgrn_kernel(*refs, reverse, final):
    if final:
        (z_ref, lb_ref, ebc_ref, msk_ref, tri_ref, blk_ref, sel_ref, oprev_ref, hh_ref, ng_ref,
         o_ref, st_scr, o_scr, qs, ks, vs, cs) = refs
    else:
        z_ref, lb_ref, ebc_ref, msk_ref, tri_ref, blk_ref, sel_ref, o_ref, st_scr, o_scr, qs, ks, vs, cs = refs

    @pl.when(pl.program_id(1) == 0)
    def _():
        st_scr[...] = jnp.zeros_like(st_scr)

    z = z_ref[...]
    lb = lb_ref[...]
    q = _silu(z[:, 0:HGRN_W])
    zf = z[:, (2 if reverse else 1) * HGRN_W:(3 if reverse else 2) * HGRN_W]
    v = z[:, 3 * HGRN_W:4 * HGRN_W]
    f = lb + (1.0 - lb) * jax.nn.sigmoid(zf)
    g = jnp.log(jnp.maximum(f, LOG_FLOOR))
    k = (1.0 - lb) * jax.nn.sigmoid(-zf)
    _gla_time_block(q, k, v, g, o_scr, st_scr, qs, ks, vs, cs, ebc_ref, msk_ref, tri_ref, blk_ref, sel_ref,
                    reverse)
    if final:
        og = z[:, 4 * HGRN_W:5 * HGRN_W]
        o_ref[...] = _head_rms_gate(oprev_ref[...] + o_scr[...], og, hh_ref, ng_ref, HGRN_DV)
    else:
        o_ref[...] = o_scr[...]


def _seq_row_map(nblk, reverse):
    if reverse:
        return lambda b, i: (b * nblk + nblk - 1 - i, 0)
    return lambda b, i: (b * nblk + i, 0)


def _const_map(b, i):
    return (0, 0)


def _lin_attn_pass(kernel, z, consts, o_prev, final_consts, bsz, t, w, reverse):
    n, zc = z.shape
    tb = min(GLA_TB, t)
    nblk = t // tb
    rmap = _seq_row_map(nblk, reverse)
    final = o_prev is not None
    r = jnp.arange(tb)
    same = (r[:, None] // GLA_CHUNK) == (r[None, :] // GLA_CHUNK)
    bound = [GLA_CHUNK * (r // GLA_CHUNK) + SUBLANES * b - (0 if reverse else 1)
             for b in range(1, GLA_CHUNK // SUBLANES)]
    sel = jnp.stack([(r[None, :] == rows[:, None]) for rows in bound]).astype(BF16)
    consts = tuple(consts) + ((same & (r[None, :] <= r[:, None])).astype(BF16), same.astype(BF16), sel)
    args = [z] + list(consts)
    in_specs = [pl.BlockSpec((tb, zc), rmap)]
    in_specs += [pl.BlockSpec(c.shape, lambda b, i, nd=c.ndim: (0,) * nd) for c in consts]
    if final:
        args += [o_prev] + list(final_consts)
        in_specs += [pl.BlockSpec((tb, MIX_W), rmap)]
        in_specs += [pl.BlockSpec(c.shape, _const_map) for c in final_consts]
    return pl.pallas_call(
        functools.partial(kernel, reverse=reverse, final=final),
        grid=(bsz, nblk),
        in_specs=in_specs,
        out_specs=pl.BlockSpec((tb, MIX_W), rmap),
        out_shape=jax.ShapeDtypeStruct((n, MIX_W), F32),
        scratch_shapes=[pltpu.VMEM((w // LANE, MIX_W * LANE // w, LANE), F32), pltpu.VMEM((tb, MIX_W), F32),
                        pltpu.VMEM((tb, w), F32), pltpu.VMEM((tb, w), F32),
                        pltpu.VMEM((tb, MIX_W), F32), pltpu.VMEM((tb, w), F32)],
        compiler_params=_params(("parallel", "arbitrary")),
    )(*args)


def _head_select(dk, heads):
    h = jnp.arange(heads)[:, None]
    lane = ((h * dk) // LANE) * LANE + jnp.arange(LANE)[None, :]
    own = (lane // dk == h).astype(BF16)
    return jnp.broadcast_to(own[:, :, None], (heads, LANE, LANE))


def _head_match(n_rows, row_blk, n_cols, col_blk, dtype):
    r = jnp.arange(n_rows) // row_blk
    c = jnp.arange(n_cols) // col_blk
    return (r[:, None] == c[None, :]).astype(dtype)


def _rwkv_prep_kernel(z_ref, zp_ref, zn_ref, muf_ref, mub_ref, w0f_ref, w2f_ref, w0b_ref, w2b_ref,
                      a0_ref, a2_ref, g2_ref, kk_ref, ka_ref, rk_ref, hh_ref,
                      r_out, kh_out, v_out, kkn_out, b_out, lwf_out, lwb_out, g_out, bonus_out):
    i = pl.program_id(1)
    last = pl.num_programs(1) - 1
    z = z_ref[...]
    tb = z.shape[0]
    rows = lax.broadcasted_iota(jnp.int32, (tb, 1), 0)
    hp = jnp.where(i == 0, 0.0, zp_ref[7:8, :])
    hn = jnp.where(i == last, 0.0, zn_ref[0:1, :])
    prev = jnp.where(rows == 0, hp, pltpu.roll(z, 1, 0))
    nxt = jnp.where(rows == tb - 1, hn, pltpu.roll(z, tb - 1, 0))
    p = z + muf_ref[...] * (prev - z) + mub_ref[...] * (nxt - z)
    r = p[:, 0:MIX_W]
    k = p[:, MIX_W:2 * MIX_W]
    v = p[:, 2 * MIX_W:3 * MIX_W]
    o = 3 * MIX_W
    wdf, wdb, ad, gd = (p[:, o + j * LANE:o + (j + 1) * LANE] for j in range(4))
    lwf = -RWKV_DECAY_SCALE * jax.nn.sigmoid(w0f_ref[...] + _dot(jnp.tanh(wdf), w2f_ref[...]))
    lwb = -RWKV_DECAY_SCALE * jax.nn.sigmoid(w0b_ref[...] + _dot(jnp.tanh(wdb), w2b_ref[...]))
    a = jax.nn.sigmoid(a0_ref[...] + _dot(ad, a2_ref[...]))
    g = _dot(jax.nn.sigmoid(gd), g2_ref[...])
    kk = k * kk_ref[...]
    ss = _dot_exact_rhs(kk * kk, hh_ref[...])
    kk = kk / jnp.maximum(jnp.sqrt(ss), 1e-12)
    kh = k * (1.0 + (a - 1.0) * ka_ref[...])
    bonus = _dot_exact_rhs(r * kh * rk_ref[...], hh_ref[...]) * v
    r_out[...] = r
    kh_out[...] = kh
    v_out[...] = v
    kkn_out[...] = kk
    b_out[...] = kk * a
    lwf_out[...] = lwf
    lwb_out[...] = lwb
    g_out[...] = g
    bonus_out[...] = bonus


def _rwkv_prep(z, consts, bsz, t):
    n, zc = z.shape
    tb = min(PREP_TB, t)
    nblk = t // tb
    hb = tb // 8
    nrow8 = n // 8
    rmap = _seq_row_map(nblk, False)
    pmap = lambda b, i: (jnp.maximum((b * nblk + i) * hb - 1, 0), 0)
    nmap = lambda b, i: (jnp.minimum((b * nblk + i + 1) * hb, nrow8 - 1), 0)
    in_specs = [pl.BlockSpec((tb, zc), rmap), pl.BlockSpec((8, zc), pmap), pl.BlockSpec((8, zc), nmap)]
    in_specs += [pl.BlockSpec(c.shape, _const_map) for c in consts]
    return pl.pallas_call(
        _rwkv_prep_kernel,
        grid=(bsz, nblk),
        in_specs=in_specs,
        out_specs=[pl.BlockSpec((tb, MIX_W), rmap)] * 9,
        out_shape=[jax.ShapeDtypeStruct((n, MIX_W), F32)] * 9,
        compiler_params=_params(("parallel", "parallel")),
    )(z, z, z, *consts)


def _rwkv_scan_kernel(r_ref, lw_ref, kh_ref, v_ref, kk_ref, b_ref, bd_ref, mstrict_ref, mincl_ref,
                      o_ref, s_scr, *, reverse):
    @pl.when(pl.program_id(1) == 0)
    def _():
        s_scr[...] = jnp.zeros_like(s_scr)

    L = RWKV_CHUNK
    tb = r_ref.shape[0]
    nchunks = tb // L
    tri = _tri_incl(L)
    lane = lax.broadcasted_iota(jnp.int32, (1, LANE), 1)
    m0 = (lane < RWKV_HEAD).astype(F32)
    m1 = 1.0 - m0
    eye = (lax.broadcasted_iota(jnp.int32, (2 * L, 2 * L), 0)
           == lax.broadcasted_iota(jnp.int32, (2 * L, 2 * L), 1)).astype(F32)
    mstrict = mstrict_ref[...]
    mincl = mincl_ref[...]
    groups = range(MIX_W // LANE)
    lanes = lambda p: slice(p * LANE, (p + 1) * LANE)
    stack2 = lambda x: jnp.concatenate([x * m0, x * m1], axis=0)

    pre = []
    for c in range(nchunks):
        sl = slice(c * L, (c + 1) * L)
        r, lw, kh, v, kk, b = (ref[sl, :] for ref in (r_ref, lw_ref, kh_ref, v_ref, kk_ref, b_ref))
        cum = _dot_exact_lhs(tri, lw)
        tot = cum[L - 1:L]
        inc = (tot - cum + lw) if reverse else cum
        exc = inc - lw
        mid = inc[L // 2:L // 2 + 1]
        e_pos = jnp.exp(inc - mid)
        e_neg = jnp.exp(mid - inc)
        e_end = jnp.exp(tot - inc)
        pre.append(dict(sl=sl, v=v, rt=r * e_pos, at=-kk * jnp.exp(exc - mid), bt=b * e_neg, kt=kh * e_neg,
                        a_abs=-kk * jnp.exp(exc), r_abs=r * jnp.exp(inc), dec=jnp.exp(tot),
                        wr=jnp.concatenate([b * e_end, kh * e_end], axis=0)))
    cells = [(c, p) for c in range(nchunks) for p in groups]
    gram = {}
    for c, p in cells:
        d = pre[c]
        lhs = jnp.concatenate([stack2(d['at'][:, lanes(p)]), stack2(d['rt'][:, lanes(p)])], axis=0)
        b_p, k_p = d['bt'][:, lanes(p)], d['kt'][:, lanes(p)]
        gram[c, p] = _dot_nt(lhs, jnp.concatenate([b_p, b_p, k_p, k_p], axis=0))
    n_ab = {k: g[0:2 * L, 0:2 * L] * mstrict for k, g in gram.items()}
    a_ak = {k: g[0:2 * L, 2 * L:4 * L] * mstrict for k, g in gram.items()}
    a_rb = {k: g[2 * L:4 * L, 0:2 * L] * mincl for k, g in gram.items()}
    a_rk = {k: g[2 * L:4 * L, 2 * L:4 * L] * mincl for k, g in gram.items()}
    v_bd = {(c, p): stack2(pre[c]['v'][:, lanes(p)]) for c, p in cells}
    x0 = {k: _dot(a_ak[k], v_bd[k]) for k in cells}
    o0 = {k: _dot(a_rk[k], v_bd[k]) for k in cells}
    tinv = {k: eye + n_ab[k] for k in cells}
    pw = dict(n_ab)
    for _ in range(5):
        pw = {k: _dot(pw[k], pw[k]) for k in cells}
        tinv = {k: tinv[k] + _dot(pw[k], tinv[k]) for k in cells}

    fold2 = lambda x: x[0:L] + x[L:2 * L]
    for c in (range(nchunks - 1, -1, -1) if reverse else range(nchunks)):
        d = pre[c]
        s = s_scr[...]
        am = _dot_nt(d['a_abs'], s)
        rm = _dot_nt(d['r_abs'], s)
        us = [_dot(tinv[c, p], stack2(am[:, lanes(p)]) + x0[c, p]) for p in groups]
        o_bd = [_dot(a_rb[c, p], us[p]) + o0[c, p] for p in groups]
        o_ref[d['sl'], :] = rm + jnp.concatenate([fold2(o) for o in o_bd], axis=1)
        u_all = jnp.concatenate([fold2(u) for u in us], axis=1)
        upd = _dot_tn(jnp.concatenate([u_all, d['v']], axis=0), d['wr'])
        s_scr[...] = s * d['dec'] + upd * bd_ref[...]


def _rwkv_scan(r, lw, kh, v, kk, b, consts, bsz, t, reverse):
    n = r.shape[0]
    tb = min(RWKV_TB, t)
    nblk = t // tb
    rmap = _seq_row_map(nblk, reverse)
    row_spec = pl.BlockSpec((tb, MIX_W), rmap)
    return pl.pallas_call(
        functools.partial(_rwkv_scan_kernel, reverse=reverse),
        grid=(bsz, nblk),
        in_specs=[row_spec] * 6 + [pl.BlockSpec(c.shape, _const_map) for c in consts],
        out_specs=row_spec,
        out_shape=jax.ShapeDtypeStruct((n, MIX_W), F32),
        scratch_shapes=[pltpu.VMEM((MIX_W, MIX_W), F32)],
        compiler_params=_params(("parallel", "arbitrary")),
    )(r, lw, kh, v, kk, b, *consts)


def _merge_kernel(x_ref, ogla_ref, ohg_ref, orf_ref, orb_ref, bonus_ref, g_ref, zg_ref,
                  wg_ref, wh_ref, wr_ref, wo_ref, ng_ref, nb_ref, hh_ref, o_ref):
    o = orf_ref[...] + orb_ref[...]
    inv = 1.0 / RWKV_HEAD
    mean = _dot_exact_rhs(o, hh_ref[...]) * inv
    d = o - mean
    var = _dot_exact_rhs(d * d, hh_ref[...]) * inv
    on = d * lax.rsqrt(var + RWKV_GN_EPS) * ng_ref[...] + nb_ref[...]
    orw = (on + bonus_ref[...]) * g_ref[...]
    zg = zg_ref[...]
    gate = lambda j: jax.nn.sigmoid(zg[:, j * D_MODEL:(j + 1) * D_MODEL])
    merged = (gate(0) * _dot(ogla_ref[...], wg_ref[...])
              + gate(1) * _dot(ohg_ref[...], wh_ref[...])
              + gate(2) * _dot(orw, wr_ref[...]))
    o_ref[...] = x_ref[...] + _dot(merged, wo_ref[...])


def _merge(x, rows, consts):
    n, d = x.shape
    tm = min(MERGE_TILE, n)
    rspec = lambda a: pl.BlockSpec((tm, a.shape[1]), lambda i: (i, 0))
    return pl.pallas_call(
        _merge_kernel,
        grid=(n // tm,),
        in_specs=[rspec(x)] + [rspec(a) for a in rows]
        + [pl.BlockSpec(c.shape, lambda i: (0, 0)) for c in consts],
        out_specs=pl.BlockSpec((tm, d), lambda i: (i, 0)),
        out_shape=jax.ShapeDtypeStruct((n, d), F32),
        compiler_params=_params(("parallel",)),
    )(x, *rows, *consts)


def _xattn_kernel(x_ref, kv_ref, g_ref, wq_ref, wo_ref, o_ref):
    x = x_ref[...]
    h = _rmsnorm(x, g_ref[...])
    q = _dot(h, wq_ref[...])
    kv = kv_ref[...]
    outs = []
    for hd in range(X_HEADS):
        ls = slice(hd * X_HEAD, (hd + 1) * X_HEAD)
        s = _dot_nt(q[:, ls], kv[:, ls]) * (X_HEAD ** -0.5)
        s = s - jnp.max(s, axis=-1, keepdims=True)
        e = jnp.exp(s)
        pr = e / jnp.sum(e, axis=-1, keepdims=True)
        outs.append(_dot(pr, kv[:, D_MODEL + hd * X_HEAD:D_MODEL + (hd + 1) * X_HEAD]))
    o_ref[...] = x + _dot(jnp.concatenate(outs, axis=1), wo_ref[...])


def _xattn(x, kv, g, wq, wo, bsz, t, n_mem):
    n, d = x.shape
    tq = min(XATTN_TQ, t)
    nblk = t // tq
    return pl.pallas_call(
        _xattn_kernel,
        grid=(bsz, nblk),
        in_specs=[pl.BlockSpec((tq, d), lambda b, i: (b * nblk + i, 0)),
                  pl.BlockSpec((n_mem, 2 * d), lambda b, i: (b, 0)),
                  pl.BlockSpec((1, d), _const_map),
                  pl.BlockSpec((d, d), _const_map),
                  pl.BlockSpec((d, d), _const_map)],
        out_specs=pl.BlockSpec((tq, d), lambda b, i: (b * nblk + i, 0)),
        out_shape=jax.ShapeDtypeStruct((n, d), F32),
        compiler_params=_params(("parallel", "parallel")),
    )(x, kv, g.reshape(1, d), wq, wo)


def _top_rows(x, n):
    nrow = x.shape[0]
    rows = lax.broadcasted_iota(jnp.int32, x.shape, 0)
    vals = []
    for _ in range(n):
        m = jnp.max(x, axis=0, keepdims=True)
        pos = jnp.min(jnp.where(x == m, rows, nrow), axis=0, keepdims=True)
        vals.append(m)
        x = jnp.where(rows == pos, -jnp.inf, x)
    return vals


def _peer_topk_kernel(q_ref, sk1_ref, sk2_ref, s1_ref, s2_ref, st_ref):
    q = q_ref[...]
    thr, mx1, mx2, rz = [], [], [], []
    for hd in range(PEER_HEADS):
        hs = slice(hd * PEER_NKEYS, (hd + 1) * PEER_NKEYS)
        qh = q[:, hd * PEER_DK:(hd + 1) * PEER_DK]
        s1 = _dot_nt_f32(sk1_ref[...], qh)
        s2 = _dot_nt_f32(sk2_ref[...], qh)
        s1_ref[hs, :] = s1
        s2_ref[hs, :] = s2
        K, half = PEER_TOPK, PEER_TOPK // 2
        v1 = _top_rows(s1, K)
        v2 = _top_rows(s2, K)
        cat = lambda rows: jnp.concatenate(rows, axis=0)
        cand = cat([v1[0] + cat(v2)] + [v1[a] + cat(v2[0:half]) for a in range(1, half)]
                   + [cat(v1[half:]) + v2[0]])
        top = _top_rows(cand, K)
        thr.append(top[K - 1])
        mx1.append(v1[0])
        mx2.append(v2[0])
        rz.append(1.0 / sum(jnp.exp(t - top[0]) for t in top))
    st_ref[...] = jnp.concatenate(thr + mx1 + mx2 + rz, axis=0)


def _peer_topk(q, sk1p, sk2p):
    n, d = q.shape
    tb = min(TOPK_TB, n)
    nrow = PEER_HEADS * PEER_NKEYS
    col = lambda r: pl.BlockSpec((r, tb), lambda i: (0, i))
    return pl.pallas_call(
        _peer_topk_kernel,
        grid=(n // tb,),
        in_specs=[pl.BlockSpec((tb, d), lambda i: (i, 0)),
                  pl.BlockSpec(sk1p.shape, lambda i: (0, 0)),
                  pl.BlockSpec(sk2p.shape, lambda i: (0, 0))],
        out_specs=[col(nrow), col(nrow), col(4 * PEER_HEADS)],
        out_shape=[jax.ShapeDtypeStruct((nrow, n), F32), jax.ShapeDtypeStruct((nrow, n), F32),
                   jax.ShapeDtypeStruct((4 * PEER_HEADS, n), F32)],
        compiler_params=_params(("parallel",)),
    )(q, sk1p, sk2p)


def _peer_expert_kernel(h_ref, u_ref, vt_ref, s1_ref, s2_ref, st_ref, x_ref, o_ref,
                        acc_scr, e1_scr, e2_scr):
    j = pl.program_id(1)
    nh, nk = PEER_HEADS, PEER_NKEYS

    @pl.when(j == 0)
    def _():
        acc_scr[...] = jnp.zeros_like(acc_scr)
        for hd in range(nh):
            hs = slice(hd * nk, (hd + 1) * nk)
            e1_scr[hs, :] = jnp.exp(s1_ref[hs, :] - st_ref[nh + hd:nh + hd + 1, :])
            e2_scr[hs, :] = (jnp.exp(s2_ref[hs, :] - st_ref[2 * nh + hd:2 * nh + hd + 1, :])
                             * st_ref[3 * nh + hd:3 * nh + hd + 1, :])

    et, tb = u_ref.shape[0], h_ref.shape[0]
    first = lambda m: _dot_nt(u_ref[m * PEER_MM1:(m + 1) * PEER_MM1, :], h_ref[...])
    per_block = PEER_MM1 // nk
    act = first(0)
    coef = []
    for il in range(et // nk):
        blk, within = divmod(il, per_block)
        if within == 0 and blk > 0:
            act = act_next
        if within == 0 and (blk + 1) * PEER_MM1 < et:
            act_next = first(blk + 1)
        i1 = j * (et // nk) + il
        parts = []
        s1_rows = [s1_ref[pl.ds(hd * nk + i1, 1), :] for hd in range(nh)]
        e1_rows = [e1_scr[pl.ds(hd * nk + i1, 1), :] for hd in range(nh)]
        for lt in range(tb // LANE):
            ts = slice(lt * LANE, (lt + 1) * LANE)
            gate = jnp.zeros((nk, LANE), F32)
            for hd in range(nh):
                hs = slice(hd * nk, (hd + 1) * nk)
                score = s1_rows[hd][:, ts] + s2_ref[hs, ts]
                weight = e1_rows[hd][:, ts] * e2_scr[hs, ts]
                gate = gate + jnp.where(score >= st_ref[hd:hd + 1, ts], weight, 0.0)
            a = act[within * nk:(within + 1) * nk, ts]
            gelu = 0.5 * a * (1.0 + lax.erf(a * (2.0 ** -0.5)))
            parts.append((gate * gelu).astype(BF16))
        coef.append(jnp.concatenate(parts, axis=1))
        if (il + 1) % (PEER_MM // nk) == 0:
            rows = slice((il + 1) * nk - PEER_MM, (il + 1) * nk)
            acc_scr[...] += jnp.dot(vt_ref[:, rows], jnp.concatenate(coef, axis=0),
                                    preferred_element_type=F32)
            coef = []

    @pl.when(j == pl.num_programs(1) - 1)
    def _():
        o_ref[...] = x_ref[...] + acc_scr[...].T


def _peer_experts(h, s1, s2, st, x, u, vt):
    n, d = x.shape
    ne = u.shape[0]
    tb = min(PEER_TB, n)
    et = PEER_ET
    nrow = s1.shape[0]
    col = lambda r: pl.BlockSpec((r, tb), lambda i, j: (0, i))
    row = pl.BlockSpec((tb, d), lambda i, j: (i, 0))
    return pl.pallas_call(
        _peer_expert_kernel,
        grid=(n // tb, ne // et),
        in_specs=[row, pl.BlockSpec((et, d), lambda i, j: (j, 0)),
                  pl.BlockSpec((d, et), lambda i, j: (0, j)),
                  col(nrow), col(nrow), col(st.shape[0]), row],
        out_specs=row,
        out_shape=jax.ShapeDtypeStruct((n, d), F32),
        scratch_shapes=[pltpu.VMEM((d, tb), F32), pltpu.VMEM((nrow, tb), F32),
                        pltpu.VMEM((nrow, tb), F32)],
        compiler_params=_params(("parallel", "arbitrary")),
    )(h, u, vt, s1, s2, st, x)


def _pad_cols(a, width):
    return jnp.pad(a, ((0, 0), (0, width - a.shape[1])))


def _pad_rows(a, height):
    return jnp.pad(a, ((0, height - a.shape[0]), (0, 0)))


def _pack_rwkv_cols(a):
    o = 3 * MIX_W
    wf = a[:, o:o + RWKV_DECAY_RANK]
    wb = a[:, o + RWKV_DECAY_RANK:o + 2 * RWKV_DECAY_RANK]
    ad = a[:, o + 2 * RWKV_DECAY_RANK:o + 2 * RWKV_DECAY_RANK + RWKV_AAA_RANK]
    gd = a[:, o + 2 * RWKV_DECAY_RANK + RWKV_AAA_RANK:]
    return jnp.concatenate([a[:, :o], _pad_cols(wf, LANE), _pad_cols(wb, LANE), _pad_cols(ad, LANE),
                            _pad_cols(gd, LANE)], axis=1)


def _layer_weights(P, l, lb):
    w_in = P['w_in'][l]
    row = lambda a: a.reshape(1, -1).astype(F32)
    W = {}
    W['norm_mix_g'] = P['norm_mix_g'][l]
    W['w_gla'] = _pad_cols(w_in[:, :GLA_IN], ZGLA_W).astype(BF16)
    W['w_hgrn'] = w_in[:, GLA_IN:GLA_IN + HGRN_IN].astype(BF16)
    W['w_rwkv'] = _pack_rwkv_cols(w_in[:, GLA_IN + HGRN_IN:GATE_OFF]).astype(BF16)
    W['w_gate'] = w_in[:, GATE_OFF:].astype(BF16)
    W['gla_up_f'] = _pad_rows(P['gla_gate_up_f'][l], LANE).astype(BF16)
    W['gla_up_b'] = _pad_rows(jnp.concatenate(
        [jnp.zeros_like(P['gla_gate_up_b'][l]), P['gla_gate_up_b'][l]], axis=0), LANE).astype(BF16)
    W['gla_bias_f'] = row(P['gla_gate_bias_f'][l])
    W['gla_bias_b'] = row(P['gla_gate_bias_b'][l])
    W['gla_norm_g'] = row(P['gla_norm_g'][l])
    W['hgrn_lb'] = row(lb)
    W['hgrn_norm_g'] = row(P['hgrn_norm_g'][l])
    W['rwkv_mu_f'] = _pack_rwkv_cols(row(P['rwkv_mu_f'][l]))
    W['rwkv_mu_b'] = _pack_rwkv_cols(row(P['rwkv_mu_b'][l]))
    W['rwkv_w0_f'] = row(P['rwkv_w0_f'][l])
    W['rwkv_w2_f'] = _pad_rows(P['rwkv_w2_f'][l], LANE).astype(BF16)
    W['rwkv_w0_b'] = row(P['rwkv_w0_b'][l])
    W['rwkv_w2_b'] = _pad_rows(P['rwkv_w2_b'][l], LANE).astype(BF16)
    W['rwkv_a0'] = row(P['rwkv_a0'][l])
    W['rwkv_a2'] = _pad_rows(P['rwkv_a2'][l], LANE).astype(BF16)
    W['rwkv_g2'] = P['rwkv_g2'][l].astype(BF16)
    for name in ('rwkv_k_k', 'rwkv_k_a', 'rwkv_r_k', 'rwkv_norm_g', 'rwkv_norm_b'):
        W[name] = row(P[name][l])
    for name in ('w_branch_gla', 'w_branch_hgrn', 'w_branch_rwkv', 'w_out', 'xattn_wq', 'xattn_wo'):
        W[name] = P[name][l].astype(BF16)
    W['peer_wq'] = P['peer_wq'][l]
    W['xattn_wkv'] = jnp.concatenate([P['xattn_wk'][l], P['xattn_wv'][l]], axis=1).astype(BF16)
    for name in ('norm_x_g', 'norm_mem_g', 'norm_ffn_g'):
        W[name] = P[name][l]
    half = PEER_DK // 2
    W['peer_sk1'] = jnp.pad(P['peer_subkeys_1'][l], ((0, 0), (0, half)))
    W['peer_sk2'] = jnp.pad(P['peer_subkeys_2'][l], ((0, 0), (half, 0)))
    W['peer_u'] = P['peer_u'][l].astype(BF16)
    W['peer_vt'] = P['peer_v'][l].astype(BF16).T
    return W


def _shared_consts():
    L = RWKV_CHUNK
    idx = jnp.arange(2 * L)
    same = (idx[:, None] // L) == (idx[None, :] // L)
    tpos = idx % L
    C = {
        'gla_ebc': _head_select(GLA_DK, GLA_HEADS),
        'gla_bdt': _head_match(MIX_W * LANE // GLA_W, GLA_DV, LANE, GLA_DK, F32),
        'hgrn_ebc': _head_select(HGRN_DK, HGRN_HEADS),
        'hgrn_bdt': _head_match(MIX_W * LANE // HGRN_W, HGRN_DV, LANE, HGRN_DK, F32),
        'hh128': _head_match(MIX_W, 128, MIX_W, 128, BF16),
        'hh64': _head_match(MIX_W, RWKV_HEAD, MIX_W, RWKV_HEAD, BF16),
        'rwkv_bd': _head_match(MIX_W, RWKV_HEAD, MIX_W, RWKV_HEAD, F32),
        'strict_f': (same & (tpos[None, :] < tpos[:, None])).astype(F32),
        'incl_f': (same & (tpos[None, :] <= tpos[:, None])).astype(F32),
        'strict_b': (same & (tpos[None, :] > tpos[:, None])).astype(F32),
        'incl_b': (same & (tpos[None, :] >= tpos[:, None])).astype(F32),
    }
    return C


def _encoder_layer(x, mem, W, C, bsz, t, n_mem):
    g_mix = W['norm_mix_g']
    z_gla = _norm_matmul(x, g_mix, W['w_gla'])
    z_hgrn = _norm_matmul(x, g_mix, W['w_hgrn'])
    z_rwkv = _norm_matmul(x, g_mix, W['w_rwkv'])
    z_gate = _norm_matmul(x, g_mix, W['w_gate'])

    gla_c = (C['gla_ebc'], C['gla_bdt'])
    o_f = _lin_attn_pass(_gla_kernel, z_gla, (W['gla_up_f'], W['gla_bias_f']) + gla_c, None, None,
                         bsz, t, GLA_W, False)
    o_gla = _lin_attn_pass(_gla_kernel, z_gla, (W['gla_up_b'], W['gla_bias_b']) + gla_c, o_f,
                           (C['hh128'], W['gla_norm_g']), bsz, t, GLA_W, True)

    hg_c = (W['hgrn_lb'], C['hgrn_ebc'], C['hgrn_bdt'])
    o_f = _lin_attn_pass(_hgrn_kernel, z_hgrn, hg_c, None, None, bsz, t, HGRN_W, False)
    o_hgrn = _lin_attn_pass(_hgrn_kernel, z_hgrn, hg_c, o_f, (C['hh128'], W['hgrn_norm_g']),
                            bsz, t, HGRN_W, True)

    prep_c = (W['rwkv_mu_f'], W['rwkv_mu_b'], W['rwkv_w0_f'], W['rwkv_w2_f'], W['rwkv_w0_b'],
              W['rwkv_w2_b'], W['rwkv_a0'], W['rwkv_a2'], W['rwkv_g2'], W['rwkv_k_k'], W['rwkv_k_a'],
              W['rwkv_r_k'], C['hh64'])
    r, kh, v, kk, b, lw_f, lw_b, g, bonus = _rwkv_prep(z_rwkv, prep_c, bsz, t)
    o_rf = _rwkv_scan(r, lw_f, kh, v, kk, b, (C['rwkv_bd'], C['strict_f'], C['incl_f']), bsz, t, False)
    o_rb = _rwkv_scan(r, lw_b, kh, v, kk, b, (C['rwkv_bd'], C['strict_b'], C['incl_b']), bsz, t, True)

    x = _merge(x, (o_gla, o_hgrn, o_rf, o_rb, bonus, g, z_gate),
               (W['w_branch_gla'], W['w_branch_hgrn'], W['w_branch_rwkv'], W['w_out'],
                W['rwkv_norm_g'], W['rwkv_norm_b'], C['hh64']))

    kv = _norm_matmul(mem, W['norm_mem_g'], W['xattn_wkv'])
    x = _xattn(x, kv, W['norm_x_g'], W['xattn_wq'], W['xattn_wo'], bsz, t, n_mem)

    q, h = _norm_matmul(x, W['norm_ffn_g'], W['peer_wq'], emit_h=True)
    s1, s2, st = _peer_topk(q, W['peer_sk1'], W['peer_sk2'])
    return _peer_experts(h, s1, s2, st, x, W['peer_u'], W['peer_vt'])


def _run_trunk(x, mem, P, weights, C):
    bsz, t, d = x.shape
    n_mem = mem.shape[1]
    x = x.reshape(bsz * t, d)
    mem = mem.reshape(bsz * n_mem, d)
    for W in weights:
        x = _encoder_layer(x, mem, W, C, bsz, t, n_mem)
    return _final_norm(x, P['final_norm_g']).reshape(bsz, t, d)


def _hgrn_lower_bounds(logits):
    sm = jax.nn.softmax(logits.astype(F32), axis=0)
    return jnp.cumsum(sm, axis=0) - sm[0]


def kernel(x_prompt, x_sample, mem_prompt, mem_sample, norm_mix_g, w_in, gla_gate_up_f, gla_gate_up_b, gla_gate_bias_f, gla_gate_bias_b, gla_norm_g, hgrn_lb_logits, hgrn_norm_g, rwkv_mu_f, rwkv_mu_b, rwkv_w0_f, rwkv_w2_f, rwkv_w0_b, rwkv_w2_b, rwkv_a0, rwkv_a2, rwkv_g2, rwkv_k_k, rwkv_k_a, rwkv_r_k, rwkv_norm_g, rwkv_norm_b, w_branch_gla, w_branch_hgrn, w_branch_rwkv, w_out, norm_x_g, norm_mem_g, xattn_wq, xattn_wk, xattn_wv, xattn_wo, norm_ffn_g, peer_wq, peer_subkeys_1, peer_subkeys_2, peer_u, peer_v, final_norm_g):
    P = dict(norm_mix_g=norm_mix_g, w_in=w_in, gla_gate_up_f=gla_gate_up_f, gla_gate_up_b=gla_gate_up_b,
             gla_gate_bias_f=gla_gate_bias_f, gla_gate_bias_b=gla_gate_bias_b, gla_norm_g=gla_norm_g,
             hgrn_lb_logits=hgrn_lb_logits, hgrn_norm_g=hgrn_norm_g, rwkv_mu_f=rwkv_mu_f, rwkv_mu_b=rwkv_mu_b,
             rwkv_w0_f=rwkv_w0_f, rwkv_w2_f=rwkv_w2_f, rwkv_w0_b=rwkv_w0_b, rwkv_w2_b=rwkv_w2_b,
             rwkv_a0=rwkv_a0, rwkv_a2=rwkv_a2, rwkv_g2=rwkv_g2, rwkv_k_k=rwkv_k_k, rwkv_k_a=rwkv_k_a,
             rwkv_r_k=rwkv_r_k, rwkv_norm_g=rwkv_norm_g, rwkv_norm_b=rwkv_norm_b, w_branch_gla=w_branch_gla,
             w_branch_hgrn=w_branch_hgrn, w_branch_rwkv=w_branch_rwkv, w_out=w_out, norm_x_g=norm_x_g,
             norm_mem_g=norm_mem_g, xattn_wq=xattn_wq, xattn_wk=xattn_wk, xattn_wv=xattn_wv, xattn_wo=xattn_wo,
             norm_ffn_g=norm_ffn_g, peer_wq=peer_wq, peer_subkeys_1=peer_subkeys_1,
             peer_subkeys_2=peer_subkeys_2, peer_u=peer_u, peer_v=peer_v, final_norm_g=final_norm_g)
    depth = w_in.shape[0]
    lbs = _hgrn_lower_bounds(hgrn_lb_logits)
    weights = [_layer_weights(P, l, lbs[l]) for l in range(depth)]
    C = _shared_consts()
    return (_run_trunk(x_prompt, mem_prompt, P, weights, C),
            _run_trunk(x_sample, mem_sample, P, weights, C))
```

```python
import functools

import jax
import jax.numpy as jnp
from jax import lax
from jax.experimental import pallas as pl
from jax.experimental.pallas import tpu as pltpu

F32 = jnp.float32
BF16 = jnp.bfloat16

D_MODEL = 1024
EPS = 1e-6
LOG_FLOOR = 1e-30
DEAD_EXPONENT = -1e30
MIX_W = 512
GLA_HEADS, GLA_DK, GLA_DV = 4, 64, 128
GLA_GATE_RANK = 16
GLA_GATE_NORM = 16.0
HGRN_HEADS, HGRN_DK, HGRN_DV = 4, 128, 128
RWKV_HEADS, RWKV_HEAD = 8, 64
RWKV_DECAY_RANK, RWKV_AAA_RANK, RWKV_GATE_RANK = 64, 64, 128
RWKV_DECAY_SCALE = 0.606531
RWKV_GN_EPS = 64e-5
X_HEADS = 4
X_HEAD = D_MODEL // X_HEADS
PEER_HEADS, PEER_DK, PEER_NKEYS, PEER_TOPK = 8, 128, 128, 16

GLA_W = GLA_HEADS * GLA_DK
GLA_IN = 2 * GLA_W + 2 * MIX_W + 2 * GLA_GATE_RANK
HGRN_W = HGRN_HEADS * HGRN_DK
HGRN_IN = 5 * MIX_W
RWKV_IN = 3 * MIX_W + 2 * RWKV_DECAY_RANK + RWKV_AAA_RANK + RWKV_GATE_RANK
GATE_OFF = GLA_IN + HGRN_IN + RWKV_IN

LANE = 128
SUBLANES = 8
ZGLA_W = 2 * GLA_W + 2 * MIX_W + LANE
ZRWKV_W = 3 * MIX_W + 4 * LANE

ROW_TILE = 512
MERGE_TILE = 256
GLA_CHUNK = 32
GLA_TB = 256
RWKV_CHUNK = 64
RWKV_TB = 256
PREP_TB = 256
XATTN_TQ = 256
TOPK_TB = 256
PEER_TB = 256
PEER_ET = 2048
PEER_MM = 1024
PEER_MM1 = 256
VMEM_LIMIT = 48 * 1024 * 1024


def _params(sem):
    return pltpu.CompilerParams(dimension_semantics=sem, vmem_limit_bytes=VMEM_LIMIT)


def _dot(a, b):
    return jnp.dot(a.astype(BF16), b.astype(BF16), preferred_element_type=F32)


def _dot_nt(a, b):
    return lax.dot_general(a.astype(BF16), b.astype(BF16), (((1,), (1,)), ((), ())),
                           preferred_element_type=F32)


def _dot_nt_f32(a, b):
    return lax.dot_general(a, b, (((1,), (1,)), ((), ())), precision=lax.Precision.HIGHEST,
                           preferred_element_type=F32)


def _dot_tn(a, b):
    return lax.dot_general(a.astype(BF16), b.astype(BF16), (((0,), (0,)), ((), ())),
                           preferred_element_type=F32)


def _split3(x):
    hi = x.astype(BF16)
    r1 = x - hi.astype(F32)
    mid = r1.astype(BF16)
    lo = (r1 - mid.astype(F32)).astype(BF16)
    return hi, mid, lo


def _dot_exact_rhs(x, m):
    hi, mid, lo = _split3(x)
    f = lambda p: jnp.dot(p, m, preferred_element_type=F32)
    return f(hi) + f(mid) + f(lo)


def _dot_exact_lhs(m, x):
    hi, mid, lo = _split3(x)
    f = lambda p: jnp.dot(m, p, preferred_element_type=F32)
    return f(hi) + f(mid) + f(lo)


def _tri_incl(n):
    r = lax.broadcasted_iota(jnp.int32, (n, n), 0)
    c = lax.broadcasted_iota(jnp.int32, (n, n), 1)
    return (r >= c).astype(BF16)


def _rmsnorm(x, g):
    return x * lax.rsqrt(jnp.mean(x * x, axis=-1, keepdims=True) + EPS) * g


def _log_sigmoid(x):
    return jnp.minimum(x, 0.0) - jnp.log(1.0 + jnp.exp(-jnp.abs(x)))


def _silu(x):
    return x * jax.nn.sigmoid(x)


def _norm_matmul_kernel(x_ref, g_ref, w_ref, o_ref, *h_ref):
    h = _rmsnorm(x_ref[...], g_ref[...])
    if w_ref.dtype == F32:
        o_ref[...] = jnp.dot(h, w_ref[...], precision=lax.Precision.HIGHEST, preferred_element_type=F32)
    else:
        o_ref[...] = jnp.dot(h.astype(BF16), w_ref[...], preferred_element_type=F32)
    if h_ref:
        h_ref[0][...] = h.astype(h_ref[0].dtype)


def _norm_matmul(x, g, w, emit_h=False):
    n, d = x.shape
    c = w.shape[1]
    tm = min(ROW_TILE, n)
    out_shape = [jax.ShapeDtypeStruct((n, c), F32)]
    out_specs = [pl.BlockSpec((tm, c), lambda i: (i, 0))]
    if emit_h:
        out_shape.append(jax.ShapeDtypeStruct((n, d), BF16))
        out_specs.append(pl.BlockSpec((tm, d), lambda i: (i, 0)))
    res = pl.pallas_call(
        _norm_matmul_kernel,
        grid=(n // tm,),
        in_specs=[pl.BlockSpec((tm, d), lambda i: (i, 0)),
                  pl.BlockSpec((1, d), lambda i: (0, 0)),
                  pl.BlockSpec((d, c), lambda i: (0, 0))],
        out_specs=out_specs,
        out_shape=out_shape,
        compiler_params=_params(("parallel",)),
    )(x, g.reshape(1, d), w)
    return res if emit_h else res[0]


def _final_norm_kernel(x_ref, g_ref, o_ref):
    o_ref[...] = _rmsnorm(x_ref[...], g_ref[...])


def _final_norm(x, g):
    n, d = x.shape
    tm = min(ROW_TILE, n)
    return pl.pallas_call(
        _final_norm_kernel,
        grid=(n // tm,),
        in_specs=[pl.BlockSpec((tm, d), lambda i: (i, 0)), pl.BlockSpec((1, d), lambda i: (0, 0))],
        out_specs=pl.BlockSpec((tm, d), lambda i: (i, 0)),
        out_shape=jax.ShapeDtypeStruct((n, d), F32),
        compiler_params=_params(("parallel",)),
    )(x, g.reshape(1, d))


def _gla_time_block(q, k, v, g, o_scr, st_scr, qs, ks, vs, cs, ebc_ref, msk_ref, tri_ref, blk_ref, sel_ref,
                    reverse):
    tb, w = q.shape
    hv = v.shape[1]
    C, S = GLA_CHUNK, SUBLANES
    nchunks, nsub = tb // C, C // S
    cum = _dot_exact_lhs(tri_ref[...], g)
    tot = _dot_exact_lhs(blk_ref[...], g)
    if reverse:
        cum = tot - cum + g
    qs[...] = q
    ks[...] = k
    vs[...] = v
    cs[...] = cum
    sub = (lax.broadcasted_iota(jnp.int32, (tb, 1), 0) % C) // S
    qf, kf = [], []
    for bd in range(1, nsub):
        ref = _dot_exact_lhs(sel_ref[bd - 1], cum)
        if reverse:
            q_rows, k_rows = sub == bd - 1, sub >= bd
        else:
            q_rows, k_rows = sub == bd, sub < bd
        qf.append(q * jnp.exp(jnp.where(q_rows, cum - ref, DEAD_EXPONENT)))
        kf.append(k * jnp.exp(jnp.where(k_rows, ref - cum, DEAD_EXPONENT)))
    same_chunk = blk_ref[...].astype(F32)
    pos = lax.broadcasted_iota(jnp.int32, (S, 1), 0)
    lane = lax.broadcasted_iota(jnp.int32, (1, LANE), 1)
    heads = hv // LANE
    dk = w // heads
    for h in range(heads):
        lt = (h * dk) // LANE
        ls = slice(lt * LANE, (lt + 1) * LANE)
        hs = slice(h * LANE, (h + 1) * LANE)
        o = jnp.zeros((tb, LANE), F32)
        for b in range(S):
            live = (pos <= b) if reverse else (pos >= b)
            w_ij, v_j = [], []
            for sb in range(tb // S):
                rs = slice(sb * S, (sb + 1) * S)
                row = slice(sb * S + b, sb * S + b + 1)
                v_j.append(vs[row, hs])
                decay = jnp.exp(jnp.where(live, cs[rs, ls] - cs[row, ls], DEAD_EXPONENT))
                w_ij.append((qs[rs, ls] * ks[row, ls] * decay).astype(BF16))
            s_j = jnp.dot(jnp.concatenate(w_ij, axis=0), ebc_ref[h], preferred_element_type=F32)
            o = o + jnp.concatenate([s_j[sb * S:(sb + 1) * S] * v_j[sb] for sb in range(tb // S)], axis=0)
        own = ((lane // dk) == (h % (LANE // dk))).astype(F32) if dk < LANE else None
        scores = jnp.zeros((tb, tb), F32)
        for bd in range(nsub - 1):
            q_h = qf[bd][:, ls] if own is None else qf[bd][:, ls] * own
            scores = scores + _dot_nt(q_h, kf[bd][:, ls])
        o_scr[:, hs] = o + _dot(scores * same_chunk, v[:, hs])
    qe = q * jnp.exp(cum)
    kd = k * jnp.exp(tot - cum)
    dec = jnp.exp(tot)
    ntile = w // LANE
    vw = hv // ntile
    for c in (range(nchunks - 1, -1, -1) if reverse else range(nchunks)):
        sl = slice(c * C, (c + 1) * C)
        st = [st_scr[p] for p in range(ntile)]
        inter = [_dot_nt(qe[sl, p * LANE:(p + 1) * LANE], st[p]) for p in range(ntile)]
        o_scr[sl, :] = o_scr[sl, :] + jnp.concatenate(inter, axis=1)
        upd = [_dot_tn(v[sl, p * vw:(p + 1) * vw], kd[sl, p * LANE:(p + 1) * LANE]) for p in range(ntile)]
        for p in range(ntile):
            if vw > LANE:
                upd[p] = upd[p] * msk_ref[...]
            st_scr[p] = st[p] * dec[c * C:c * C + 1, p * LANE:(p + 1) * LANE] + upd[p]


def _head_rms_gate(o, og, hh_ref, ng_ref, dv):
    ms = _dot_exact_rhs(o * o, hh_ref[...]) * (1.0 / dv)
    return o * lax.rsqrt(ms + EPS) * ng_ref[...] * _silu(og)


def _gla_kernel(*refs, reverse, final):
    if final:
        (z_ref, up_ref, bias_ref, ebc_ref, msk_ref, tri_ref, blk_ref, sel_ref, oprev_ref, hh_ref, ng_ref,
         o_ref, st_scr, o_scr, qs, ks, vs, cs) = refs
    else:
        (z_ref, up_ref, bias_ref, ebc_ref, msk_ref, tri_ref, blk_ref, sel_ref,
         o_ref, st_scr, o_scr, qs, ks, vs, cs) = refs

    @pl.when(pl.program_id(1) == 0)
    def _():
        st_scr[...] = jnp.zeros_like(st_scr)

    z = z_ref[...]
    q = z[:, 0:GLA_W] * (GLA_DK ** -0.5)
    k = z[:, GLA_W:2 * GLA_W]
    v = z[:, 2 * GLA_W:2 * GLA_W + MIX_W]
    gd = z[:, 2 * GLA_W + 2 * MIX_W:ZGLA_W]
    g = _log_sigmoid(_dot(gd, up_ref[...]) + bias_ref[...]) * (1.0 / GLA_GATE_NORM)
    _gla_time_block(q, k, v, g, o_scr, st_scr, qs, ks, vs, cs, ebc_ref, msk_ref, tri_ref, blk_ref, sel_ref,
                    reverse)
    if final:
        og = z[:, 2 * GLA_W + MIX_W:2 * GLA_W + 2 * MIX_W]
        o_ref[...] = _head_rms_gate(oprev_ref[...] + o_scr[...], og, hh_ref, ng_ref, GLA_DV)
    else:
        o_ref[...] = o_scr[...]


def _hgrn_kernel(*refs, reverse, final):
    if final:
        (z_ref, lb_ref, ebc_ref, msk_ref, tri_ref, blk_ref, sel_ref, oprev_ref, hh_ref, ng_ref,
         o_ref, st_scr, o_scr, qs, ks, vs, cs) = refs
    else:
        z_ref, lb_ref, ebc_ref, msk_ref, tri_ref, blk_ref, sel_ref, o_ref, st_scr, o_scr, qs, ks, vs, cs = refs

    @pl.when(pl.program_id(1) == 0)
    def _():
        st_scr[...] = jnp.zeros_like(st_scr)

    z = z_ref[...]
    lb = lb_ref[...]
    q = _silu(z[:, 0:HGRN_W])
    zf = z[:, (2 if reverse else 1) * HGRN_W:(3 if reverse else 2) * HGRN_W]
    v = z[:, 3 * HGRN_W:4 * HGRN_W]
    f = lb + (1.0 - lb) * jax.nn.sigmoid(zf)
    g = jnp.log(jnp.maximum(f, LOG_FLOOR))
    k = (1.0 - lb) * jax.nn.sigmoid(-zf)
    _gla_time_block(q, k, v, g, o_scr, st_scr, qs, ks, vs, cs, ebc_ref, msk_ref, tri_ref, blk_ref, sel_ref,
                    reverse)
    if final:
        og = z[:, 4 * HGRN_W:5 * HGRN_W]
        o_ref[...] = _head_rms_gate(oprev_ref[...] + o_scr[...], og, hh_ref, ng_ref, HGRN_DV)
    else:
        o_ref[...] = o_scr[...]


def _seq_row_map(nblk, reverse):
    if reverse:
        return lambda b, i: (b * nblk + nblk - 1 - i, 0)
    return lambda b, i: (b * nblk + i, 0)


def _const_map(b, i):
    return (0, 0)


def _lin_attn_pass(kernel, z, consts, o_prev, final_consts, bsz, t, w, reverse):
    n, zc = z.shape
    tb = min(GLA_TB, t)
    nblk = t // tb
    rmap = _seq_row_map(nblk, reverse)
    final = o_prev is not None
    r = jnp.arange(tb)
    same = (r[:, None] // GLA_CHUNK) == (r[None, :] // GLA_CHUNK)
    bound = [GLA_CHUNK * (r // GLA_CHUNK) + SUBLANES * b - (0 if reverse else 1)
             for b in range(1, GLA_CHUNK // SUBLANES)]
    sel = jnp.stack([(r[None, :] == rows[:, None]) for rows in bound]).astype(BF16)
    consts = tuple(consts) + ((same & (r[None, :] <= r[:, None])).astype(BF16), same.astype(BF16), sel)
    args = [z] + list(consts)
    in_specs = [pl.BlockSpec((tb, zc), rmap)]
    in_specs += [pl.BlockSpec(c.shape, lambda b, i, nd=c.ndim: (0,) * nd) for c in consts]
    if final:
        args += [o_prev] + list(final_consts)
        in_specs += [pl.BlockSpec((tb, MIX_W), rmap)]
        in_specs += [pl.BlockSpec(c.shape, _const_map) for c in final_consts]
    return pl.pallas_call(
        functools.partial(kernel, reverse=reverse, final=final),
        grid=(bsz, nblk),
        in_specs=in_specs,
        out_specs=pl.BlockSpec((tb, MIX_W), rmap),
        out_shape=jax.ShapeDtypeStruct((n, MIX_W), F32),
        scratch_shapes=[pltpu.VMEM((w // LANE, MIX_W * LANE // w, LANE), F32), pltpu.VMEM((tb, MIX_W), F32),
                        pltpu.VMEM((tb, w), F32), pltpu.VMEM((tb, w), F32),
                        pltpu.VMEM((tb, MIX_W), F32), pltpu.VMEM((tb, w), F32)],
        compiler_params=_params(("parallel", "arbitrary")),
    )(*args)


def _head_select(dk, heads):
    h = jnp.arange(heads)[:, None]
    lane = ((h * dk) // LANE) * LANE + jnp.arange(LANE)[None, :]
    own = (lane // dk == h).astype(BF16)
    return jnp.broadcast_to(own[:, :, None], (heads, LANE, LANE))


def _head_match(n_rows, row_blk, n_cols, col_blk, dtype):
    r = jnp.arange(n_rows) // row_blk
    c = jnp.arange(n_cols) // col_blk
    return (r[:, None] == c[None, :]).astype(dtype)


def _rwkv_prep_kernel(z_ref, zp_ref, zn_ref, muf_ref, mub_ref, w0f_ref, w2f_ref, w0b_ref, w2b_ref,
                      a0_ref, a2_ref, g2_ref, kk_ref, ka_ref, rk_ref, hh_ref,
                      r_out, kh_out, v_out, kkn_out, b_out, lwf_out, lwb_out, g_out, bonus_out):
    i = pl.program_id(1)
    last = pl.num_programs(1) - 1
    z = z_ref[...]
    tb = z.shape[0]
    rows = lax.broadcasted_iota(jnp.int32, (tb, 1), 0)
    hp = jnp.where(i == 0, 0.0, zp_ref[7:8, :])
    hn = jnp.where(i == last, 0.0, zn_ref[0:1, :])
    prev = jnp.where(rows == 0, hp, pltpu.roll(z, 1, 0))
    nxt = jnp.where(rows == tb - 1, hn, pltpu.roll(z, tb - 1, 0))
    p = z + muf_ref[...] * (prev - z) + mub_ref[...] * (nxt - z)
    r = p[:, 0:MIX_W]
    k = p[:, MIX_W:2 * MIX_W]
    v = p[:, 2 * MIX_W:3 * MIX_W]
    o = 3 * MIX_W
    wdf, wdb, ad, gd = (p[:, o + j * LANE:o + (j + 1) * LANE] for j in range(4))
    lwf = -RWKV_DECAY_SCALE * jax.nn.sigmoid(w0f_ref[...] + _dot(jnp.tanh(wdf), w2f_ref[...]))
    lwb = -RWKV_DECAY_SCALE * jax.nn.sigmoid(w0b_ref[...] + _dot(jnp.tanh(wdb), w2b_ref[...]))
    a = jax.nn.sigmoid(a0_ref[...] + _dot(ad, a2_ref[...]))
    g = _dot(jax.nn.sigmoid(gd), g2_ref[...])
    kk = k * kk_ref[...]
    ss = _dot_exact_rhs(kk * kk, hh_ref[...])
    kk = kk / jnp.maximum(jnp.sqrt(ss), 1e-12)
    kh = k * (1.0 + (a - 1.0) * ka_ref[...])
    bonus = _dot_exact_rhs(r * kh * rk_ref[...], hh_ref[...]) * v
    r_out[...] = r
    kh_out[...] = kh
    v_out[...] = v
    kkn_out[...] = kk
    b_out[...] = kk * a
    lwf_out[...] = lwf
    lwb_out[...] = lwb
    g_out[...] = g
    bonus_out[...] = bonus


def _rwkv_prep(z, consts, bsz, t):
    n, zc = z.shape
    tb = min(PREP_TB, t)
    nblk = t // tb
    hb = tb // 8
    nrow8 = n // 8
    rmap = _seq_row_map(nblk, False)
    pmap = lambda b, i: (jnp.maximum((b * nblk + i) * hb - 1, 0), 0)
    nmap = lambda b, i: (jnp.minimum((b * nblk + i + 1) * hb, nrow8 - 1), 0)
    in_specs = [pl.BlockSpec((tb, zc), rmap), pl.BlockSpec((8, zc), pmap), pl.BlockSpec((8, zc), nmap)]
    in_specs += [pl.BlockSpec(c.shape, _const_map) for c in consts]
    return pl.pallas_call(
        _rwkv_prep_kernel,
        grid=(bsz, nblk),
        in_specs=in_specs,
        out_specs=[pl.BlockSpec((tb, MIX_W), rmap)] * 9,
        out_shape=[jax.ShapeDtypeStruct((n, MIX_W), F32)] * 9,
        compiler_params=_params(("parallel", "parallel")),
    )(z, z, z, *consts)


def _rwkv_scan_kernel(r_ref, lw_ref, kh_ref, v_ref, kk_ref, b_ref, bd_ref, mstrict_ref, mincl_ref,
                      o_ref, s_scr, *, reverse):
    @pl.when(pl.program_id(1) == 0)
    def _():
        s_scr[...] = jnp.zeros_like(s_scr)

    L = RWKV_CHUNK
    tb = r_ref.shape[0]
    nchunks = tb // L
    tri = _tri_incl(L)
    lane = lax.broadcasted_iota(jnp.int32, (1, LANE), 1)
    m0 = (lane < RWKV_HEAD).astype(F32)
    m1 = 1.0 - m0
    eye = (lax.broadcasted_iota(jnp.int32, (2 * L, 2 * L), 0)
           == lax.broadcasted_iota(jnp.int32, (2 * L, 2 * L), 1)).astype(F32)
    mstrict = mstrict_ref[...]
    mincl = mincl_ref[...]
    groups = range(MIX_W // LANE)
    lanes = lambda p: slice(p * LANE, (p + 1) * LANE)
    stack2 = lambda x: jnp.concatenate([x * m0, x * m1], axis=0)

    pre = []
    for c in range(nchunks):
        sl = slice(c * L, (c + 1) * L)
        r, lw, kh, v, kk, b = (ref[sl, :] for ref in (r_ref, lw_ref, kh_ref, v_ref, kk_ref, b_ref))
        cum = _dot_exact_lhs(tri, lw)
        tot = cum[L - 1:L]
        inc = (tot - cum + lw) if reverse else cum
        exc = inc - lw
        mid = inc[L // 2:L // 2 + 1]
        e_pos = jnp.exp(inc - mid)
        e_neg = jnp.exp(mid - inc)
        e_end = jnp.exp(tot - inc)
        pre.append(dict(sl=sl, v=v, rt=r * e_pos, at=-kk * jnp.exp(exc - mid), bt=b * e_neg, kt=kh * e_neg,
                        a_abs=-kk * jnp.exp(exc), r_abs=r * jnp.exp(inc), dec=jnp.exp(tot),
                        wr=jnp.concatenate([b * e_end, kh * e_end], axis=0)))
    cells = [(c, p) for c in range(nchunks) for p in groups]
    gram = {}
    for c, p in cells:
        d = pre[c]
        lhs = jnp.concatenate([stack2(d['at'][:, lanes(p)]), stack2(d['rt'][:, lanes(p)])], axis=0)
        b_p, k_p = d['bt'][:, lanes(p)], d['kt'][:, lanes(p)]
        gram[c, p] = _dot_nt(lhs, jnp.concatenate([b_p, b_p, k_p, k_p], axis=0))
    n_ab = {k: g[0:2 * L, 0:2 * L] * mstrict for k, g in gram.items()}
    a_ak = {k: g[0:2 * L, 2 * L:4 * L] * mstrict for k, g in gram.items()}
    a_rb = {k: g[2 * L:4 * L, 0:2 * L] * mincl for k, g in gram.items()}
    a_rk = {k: g[2 * L:4 * L, 2 * L:4 * L] * mincl for k, g in gram.items()}
    v_bd = {(c, p): stack2(pre[c]['v'][:, lanes(p)]) for c, p in cells}
    x0 = {k: _dot(a_ak[k], v_bd[k]) for k in cells}
    o0 = {k: _dot(a_rk[k], v_bd[k]) for k in cells}
    tinv = {k: eye + n_ab[k] for k in cells}
    pw = dict(n_ab)
    for _ in range(5):
        pw = {k: _dot(pw[k], pw[k]) for k in cells}
        tinv = {k: tinv[k] + _dot(pw[k], tinv[k]) for k in cells}

    fold2 = lambda x: x[0:L] + x[L:2 * L]
    for c in (range(nchunks - 1, -1, -1) if reverse else range(nchunks)):
        d = pre[c]
        s = s_scr[...]
        am = _dot_nt(d['a_abs'], s)
        rm = _dot_nt(d['r_abs'], s)
        us = [_dot(tinv[c, p], stack2(am[:, lanes(p)]) + x0[c, p]) for p in groups]
        o_bd = [_dot(a_rb[c, p], us[p]) + o0[c, p] for p in groups]
        o_ref[d['sl'], :] = rm + jnp.concatenate([fold2(o) for o in o_bd], axis=1)
        u_all = jnp.concatenate([fold2(u) for u in us], axis=1)
        upd = _dot_tn(jnp.concatenate([u_all, d['v']], axis=0), d['wr'])
        s_scr[...] = s * d['dec'] + upd * bd_ref[...]


def _rwkv_scan(r, lw, kh, v, kk, b, consts, bsz, t, reverse):
    n = r.shape[0]
    tb = min(RWKV_TB, t)
    nblk = t // tb
    rmap = _seq_row_map(nblk, reverse)
    row_spec = pl.BlockSpec((tb, MIX_W), rmap)
    return pl.pallas_call(
        functools.partial(_rwkv_scan_kernel, reverse=reverse),
        grid=(bsz, nblk),
        in_specs=[row_spec] * 6 + [pl.BlockSpec(c.shape, _const_map) for c in consts],
        out_specs=row_spec,
        out_shape=jax.ShapeDtypeStruct((n, MIX_W), F32),
        scratch_shapes=[pltpu.VMEM((MIX_W, MIX_W), F32)],
        compiler_params=_params(("parallel", "arbitrary")),
    )(r, lw, kh, v, kk, b, *consts)


def _merge_kernel(x_ref, ogla_ref, ohg_ref, orf_ref, orb_ref, bonus_ref, g_ref, zg_ref,
                  wg_ref, wh_ref, wr_ref, wo_ref, ng_ref, nb_ref, hh_ref, o_ref):
    o = orf_ref[...] + orb_ref[...]
    inv = 1.0 / RWKV_HEAD
    mean = _dot_exact_rhs(o, hh_ref[...]) * inv
    d = o - mean
    var = _dot_exact_rhs(d * d, hh_ref[...]) * inv
    on = d * lax.rsqrt(var + RWKV_GN_EPS) * ng_ref[...] + nb_ref[...]
    orw = (on + bonus_ref[...]) * g_ref[...]
    zg = zg_ref[...]
    gate = lambda j: jax.nn.sigmoid(zg[:, j * D_MODEL:(j + 1) * D_MODEL])
    merged = (gate(0) * _dot(ogla_ref[...], wg_ref[...])
              + gate(1) * _dot(ohg_ref[...], wh_ref[...])
              + gate(2) * _dot(orw, wr_ref[...]))
    o_ref[...] = x_ref[...] + _dot(merged, wo_ref[...])


def _merge(x, rows, consts):
    n, d = x.shape
    tm = min(MERGE_TILE, n)
    rspec = lambda a: pl.BlockSpec((tm, a.shape[1]), lambda i: (i, 0))
    return pl.pallas_call(
        _merge_kernel,
        grid=(n // tm,),
        in_specs=[rspec(x)] + [rspec(a) for a in rows]
        + [pl.BlockSpec(c.shape, lambda i: (0, 0)) for c in consts],
        out_specs=pl.BlockSpec((tm, d), lambda i: (i, 0)),
        out_shape=jax.ShapeDtypeStruct((n, d), F32),
        compiler_params=_params(("parallel",)),
    )(x, *rows, *consts)


def _xattn_kernel(x_ref, kv_ref, g_ref, wq_ref, wo_ref, o_ref):
    x = x_ref[...]
    h = _rmsnorm(x, g_ref[...])
    q = _dot(h, wq_ref[...])
    kv = kv_ref[...]
    outs = []
    for hd in range(X_HEADS):
        ls = slice(hd * X_HEAD, (hd + 1) * X_HEAD)
        s = _dot_nt(q[:, ls], kv[:, ls]) * (X_HEAD ** -0.5)
        s = s - jnp.max(s, axis=-1, keepdims=True)
        e = jnp.exp(s)
        pr = e / jnp.sum(e, axis=-1, keepdims=True)
        outs.append(_dot(pr, kv[:, D_MODEL + hd * X_HEAD:D_MODEL + (hd + 1) * X_HEAD]))
    o_ref[...] = x + _dot(jnp.concatenate(outs, axis=1), wo_ref[...])


def _xattn(x, kv, g, wq, wo, bsz, t, n_mem):
    n, d = x.shape
    tq = min(XATTN_TQ, t)
    nblk = t // tq
    return pl.pallas_call(
        _xattn_kernel,
        grid=(bsz, nblk),
        in_specs=[pl.BlockSpec((tq, d), lambda b, i: (b * nblk + i, 0)),
                  pl.BlockSpec((n_mem, 2 * d), lambda b, i: (b, 0)),
                  pl.BlockSpec((1, d), _const_map),
                  pl.BlockSpec((d, d), _const_map),
                  pl.BlockSpec((d, d), _const_map)],
        out_specs=pl.BlockSpec((tq, d), lambda b, i: (b * nblk + i, 0)),
        out_shape=jax.ShapeDtypeStruct((n, d), F32),
        compiler_params=_params(("parallel", "parallel")),
    )(x, kv, g.reshape(1, d), wq, wo)


def _top_rows(x, n):
    nrow = x.shape[0]
    rows = lax.broadcasted_iota(jnp.int32, x.shape, 0)
    vals = []
    for _ in range(n):
        m = jnp.max(x, axis=0, keepdims=True)
        pos = jnp.min(jnp.where(x == m, rows, nrow), axis=0, keepdims=True)
        vals.append(m)
        x = jnp.where(rows == pos, -jnp.inf, x)
    return vals


def _peer_topk_kernel(q_ref, sk1_ref, sk2_ref, s1_ref, s2_ref, st_ref):
    q = q_ref[...]
    thr, mx1, mx2, rz = [], [], [], []
    for hd in range(PEER_HEADS):
        hs = slice(hd * PEER_NKEYS, (hd + 1) * PEER_NKEYS)
        qh = q[:, hd * PEER_DK:(hd + 1) * PEER_DK]
        s1 = _dot_nt_f32(sk1_ref[...], qh)
        s2 = _dot_nt_f32(sk2_ref[...], qh)
        s1_ref[hs, :] = s1
        s2_ref[hs, :] = s2
        K, half = PEER_TOPK, PEER_TOPK // 2
        v1 = _top_rows(s1, K + 1)
        v2 = _top_rows(s2, K + 1)
        cat = lambda rows: jnp.concatenate(rows, axis=0)
        cand = cat([v1[0] + cat(v2[0:K])] + [v1[a] + cat(v2[0:half]) for a in range(1, half)]
                   + [cat(v1[half:K]) + v2[0], v1[K] + cat(v2[0:half]), cat(v1[0:half]) + v2[K]])
        top = _top_rows(cand, K + 1)
        thr.append(0.5 * (top[K - 1] + top[K]))
        mx1.append(v1[0])
        mx2.append(v2[0])
        rz.append(1.0 / sum(jnp.exp(t - top[0]) for t in top[0:K]))
    st_ref[...] = jnp.concatenate(thr + mx1 + mx2 + rz, axis=0)


def _peer_topk(q, sk1p, sk2p):
    n, d = q.shape
    tb = min(TOPK_TB, n)
    nrow = PEER_HEADS * PEER_NKEYS
    col = lambda r: pl.BlockSpec((r, tb), lambda i: (0, i))
    return pl.pallas_call(
        _peer_topk_kernel,
        grid=(n // tb,),
        in_specs=[pl.BlockSpec((tb, d), lambda i: (i, 0)),
                  pl.BlockSpec(sk1p.shape, lambda i: (0, 0)),
                  pl.BlockSpec(sk2p.shape, lambda i: (0, 0))],
        out_specs=[col(nrow), col(nrow), col(4 * PEER_HEADS)],
        out_shape=[jax.ShapeDtypeStruct((nrow, n), F32), jax.ShapeDtypeStruct((nrow, n), F32),
                   jax.ShapeDtypeStruct((4 * PEER_HEADS, n), F32)],
        compiler_params=_params(("parallel",)),
    )(q, sk1p, sk2p)


def _peer_expert_kernel(h_ref, u_ref, vt_ref, s1_ref, s2_ref, st_ref, x_ref, o_ref,
                        acc_scr, e1_scr, e2_scr, t2_scr):
    j = pl.program_id(1)
    nh, nk = PEER_HEADS, PEER_NKEYS

    @pl.when(j == 0)
    def _():
        acc_scr[...] = jnp.zeros_like(acc_scr)
        for hd in range(nh):
            hs = slice(hd * nk, (hd + 1) * nk)
            e1_scr[hs, :] = jnp.exp(s1_ref[hs, :] - st_ref[nh + hd:nh + hd + 1, :])
            e2_scr[hs, :] = (jnp.exp(s2_ref[hs, :] - st_ref[2 * nh + hd:2 * nh + hd + 1, :])
                             * st_ref[3 * nh + hd:3 * nh + hd + 1, :])
            t2_scr[hs, :] = st_ref[hd:hd + 1, :] - s2_ref[hs, :]

    et, tb = u_ref.shape[0], h_ref.shape[0]
    first = lambda m: _dot_nt(u_ref[m * PEER_MM1:(m + 1) * PEER_MM1, :], h_ref[...])
    per_block = PEER_MM1 // nk
    act = first(0)
    coef = []
    for il in range(et // nk):
        blk, within = divmod(il, per_block)
        if within == 0 and blk > 0:
            act = act_next
        if within == 0 and (blk + 1) * PEER_MM1 < et:
            act_next = first(blk + 1)
        i1 = j * (et // nk) + il
        parts = []
        s1_rows = [s1_ref[pl.ds(hd * nk + i1, 1), :] for hd in range(nh)]
        e1_rows = [e1_scr[pl.ds(hd * nk + i1, 1), :] for hd in range(nh)]
        for lt in range(tb // LANE):
            ts = slice(lt * LANE, (lt + 1) * LANE)
            gate = jnp.zeros((nk, LANE), F32)
            for hd in range(nh):
                hs = slice(hd * nk, (hd + 1) * nk)
                weight = e1_rows[hd][:, ts] * e2_scr[hs, ts]
                gate = gate + jnp.where(s1_rows[hd][:, ts] >= t2_scr[hs, ts], weight, 0.0)
            a = act[within * nk:(within + 1) * nk, ts]
            gelu = 0.5 * a * (1.0 + lax.erf(a * (2.0 ** -0.5)))
            parts.append((gate * gelu).astype(BF16))
        coef.append(jnp.concatenate(parts, axis=1))
        if (il + 1) % (PEER_MM // nk) == 0:
            rows = slice((il + 1) * nk - PEER_MM, (il + 1) * nk)
            acc_scr[...] += jnp.dot(vt_ref[:, rows], jnp.concatenate(coef, axis=0),
                                    preferred_element_type=F32)
            coef = []

    @pl.when(j == pl.num_programs(1) - 1)
    def _():
        o_ref[...] = x_ref[...] + acc_scr[...].T


def _peer_experts(h, s1, s2, st, x, u, vt):
    n, d = x.shape
    ne = u.shape[0]
    tb = min(PEER_TB, n)
    et = PEER_ET
    nrow = s1.shape[0]
    col = lambda r: pl.BlockSpec((r, tb), lambda i, j: (0, i))
    row = pl.BlockSpec((tb, d), lambda i, j: (i, 0))
    return pl.pallas_call(
        _peer_expert_kernel,
        grid=(n // tb, ne // et),
        in_specs=[row, pl.BlockSpec((et, d), lambda i, j: (j, 0)),
                  pl.BlockSpec((d, et), lambda i, j: (0, j)),
                  col(nrow), col(nrow), col(st.shape[0]), row],
        out_specs=row,
        out_shape=jax.ShapeDtypeStruct((n, d), F32),
        scratch_shapes=[pltpu.VMEM((d, tb), F32), pltpu.VMEM((nrow, tb), F32),
                        pltpu.VMEM((nrow, tb), F32), pltpu.VMEM((nrow, tb), F32)],
        compiler_params=_params(("parallel", "arbitrary")),
    )(h, u, vt, s1, s2, st, x)


def _pad_cols(a, width):
    return jnp.pad(a, ((0, 0), (0, width - a.shape[1])))


def _pad_rows(a, height):
    return jnp.pad(a, ((0, height - a.shape[0]), (0, 0)))


def _pack_rwkv_cols(a):
    o = 3 * MIX_W
    wf = a[:, o:o + RWKV_DECAY_RANK]
    wb = a[:, o + RWKV_DECAY_RANK:o + 2 * RWKV_DECAY_RANK]
    ad = a[:, o + 2 * RWKV_DECAY_RANK:o + 2 * RWKV_DECAY_RANK + RWKV_AAA_RANK]
    gd = a[:, o + 2 * RWKV_DECAY_RANK + RWKV_AAA_RANK:]
    return jnp.concatenate([a[:, :o], _pad_cols(wf, LANE), _pad_cols(wb, LANE), _pad_cols(ad, LANE),
                            _pad_cols(gd, LANE)], axis=1)


def _layer_weights(P, l, lb):
    w_in = P['w_in'][l]
    row = lambda a: a.reshape(1, -1).astype(F32)
    W = {}
    W['norm_mix_g'] = P['norm_mix_g'][l]
    W['w_gla'] = _pad_cols(w_in[:, :GLA_IN], ZGLA_W).astype(BF16)
    W['w_hgrn'] = w_in[:, GLA_IN:GLA_IN + HGRN_IN].astype(BF16)
    W['w_rwkv'] = _pack_rwkv_cols(w_in[:, GLA_IN + HGRN_IN:GATE_OFF]).astype(BF16)
    W['w_gate'] = w_in[:, GATE_OFF:].astype(BF16)
    W['gla_up_f'] = _pad_rows(P['gla_gate_up_f'][l], LANE).astype(BF16)
    W['gla_up_b'] = _pad_rows(jnp.concatenate(
        [jnp.zeros_like(P['gla_gate_up_b'][l]), P['gla_gate_up_b'][l]], axis=0), LANE).astype(BF16)
    W['gla_bias_f'] = row(P['gla_gate_bias_f'][l])
    W['gla_bias_b'] = row(P['gla_gate_bias_b'][l])
    W['gla_norm_g'] = row(P['gla_norm_g'][l])
    W['hgrn_lb'] = row(lb)
    W['hgrn_norm_g'] = row(P['hgrn_norm_g'][l])
    W['rwkv_mu_f'] = _pack_rwkv_cols(row(P['rwkv_mu_f'][l]))
    W['rwkv_mu_b'] = _pack_rwkv_cols(row(P['rwkv_mu_b'][l]))
    W['rwkv_w0_f'] = row(P['rwkv_w0_f'][l])
    W['rwkv_w2_f'] = _pad_rows(P['rwkv_w2_f'][l], LANE).astype(BF16)
    W['rwkv_w0_b'] = row(P['rwkv_w0_b'][l])
    W['rwkv_w2_b'] = _pad_rows(P['rwkv_w2_b'][l], LANE).astype(BF16)
    W['rwkv_a0'] = row(P['rwkv_a0'][l])
    W['rwkv_a2'] = _pad_rows(P['rwkv_a2'][l], LANE).astype(BF16)
    W['rwkv_g2'] = P['rwkv_g2'][l].astype(BF16)
    for name in ('rwkv_k_k', 'rwkv_k_a', 'rwkv_r_k', 'rwkv_norm_g', 'rwkv_norm_b'):
        W[name] = row(P[name][l])
    for name in ('w_branch_gla', 'w_branch_hgrn', 'w_branch_rwkv', 'w_out', 'xattn_wq', 'xattn_wo'):
        W[name] = P[name][l].astype(BF16)
    W['peer_wq'] = P['peer_wq'][l]
    W['xattn_wkv'] = jnp.concatenate([P['xattn_wk'][l], P['xattn_wv'][l]], axis=1).astype(BF16)
    for name in ('norm_x_g', 'norm_mem_g', 'norm_ffn_g'):
        W[name] = P[name][l]
    half = PEER_DK // 2
    W['peer_sk1'] = jnp.pad(P['peer_subkeys_1'][l], ((0, 0), (0, half)))
    W['peer_sk2'] = jnp.pad(P['peer_subkeys_2'][l], ((0, 0), (half, 0)))
    W['peer_u'] = P['peer_u'][l].astype(BF16)
    W['peer_vt'] = P['peer_v'][l].astype(BF16).T
    return W


def _shared_consts():
    L = RWKV_CHUNK
    idx = jnp.arange(2 * L)
    same = (idx[:, None] // L) == (idx[None, :] // L)
    tpos = idx % L
    C = {
        'gla_ebc': _head_select(GLA_DK, GLA_HEADS),
        'gla_bdt': _head_match(MIX_W * LANE // GLA_W, GLA_DV, LANE, GLA_DK, F32),
        'hgrn_ebc': _head_select(HGRN_DK, HGRN_HEADS),
        'hgrn_bdt': _head_match(MIX_W * LANE // HGRN_W, HGRN_DV, LANE, HGRN_DK, F32),
        'hh128': _head_match(MIX_W, 128, MIX_W, 128, BF16),
        'hh64': _head_match(MIX_W, RWKV_HEAD, MIX_W, RWKV_HEAD, BF16),
        'rwkv_bd': _head_match(MIX_W, RWKV_HEAD, MIX_W, RWKV_HEAD, F32),
        'strict_f': (same & (tpos[None, :] < tpos[:, None])).astype(F32),
        'incl_f': (same & (tpos[None, :] <= tpos[:, None])).astype(F32),
        'strict_b': (same & (tpos[None, :] > tpos[:, None])).astype(F32),
        'incl_b': (same & (tpos[None, :] >= tpos[:, None])).astype(F32),
    }
    return C


def _encoder_layer(x, mem, W, C, bsz, t, n_mem):
    g_mix = W['norm_mix_g']
    z_gla = _norm_matmul(x, g_mix, W['w_gla'])
    z_hgrn = _norm_matmul(x, g_mix, W['w_hgrn'])
    z_rwkv = _norm_matmul(x, g_mix, W['w_rwkv'])
    z_gate = _norm_matmul(x, g_mix, W['w_gate'])

    gla_c = (C['gla_ebc'], C['gla_bdt'])
    o_f = _lin_attn_pass(_gla_kernel, z_gla, (W['gla_up_f'], W['gla_bias_f']) + gla_c, None, None,
                         bsz, t, GLA_W, False)
    o_gla = _lin_attn_pass(_gla_kernel, z_gla, (W['gla_up_b'], W['gla_bias_b']) + gla_c, o_f,
                           (C['hh128'], W['gla_norm_g']), bsz, t, GLA_W, True)

    hg_c = (W['hgrn_lb'], C['hgrn_ebc'], C['hgrn_bdt'])
    o_f = _lin_attn_pass(_hgrn_kernel, z_hgrn, hg_c, None, None, bsz, t, HGRN_W, False)
    o_hgrn = _lin_attn_pass(_hgrn_kernel, z_hgrn, hg_c, o_f, (C['hh128'], W['hgrn_norm_g']),
                            bsz, t, HGRN_W, True)

    prep_c = (W['rwkv_mu_f'], W['rwkv_mu_b'], W['rwkv_w0_f'], W['rwkv_w2_f'], W['rwkv_w0_b'],
              W['rwkv_w2_b'], W['rwkv_a0'], W['rwkv_a2'], W['rwkv_g2'], W['rwkv_k_k'], W['rwkv_k_a'],
              W['rwkv_r_k'], C['hh64'])
    r, kh, v, kk, b, lw_f, lw_b, g, bonus = _rwkv_prep(z_rwkv, prep_c, bsz, t)
    o_rf = _rwkv_scan(r, lw_f, kh, v, kk, b, (C['rwkv_bd'], C['strict_f'], C['incl_f']), bsz, t, False)
    o_rb = _rwkv_scan(r, lw_b, kh, v, kk, b, (C['rwkv_bd'], C['strict_b'], C['incl_b']), bsz, t, True)

    x = _merge(x, (o_gla, o_hgrn, o_rf, o_rb, bonus, g, z_gate),
               (W['w_branch_gla'], W['w_branch_hgrn'], W['w_branch_rwkv'], W['w_out'],
                W['rwkv_norm_g'], W['rwkv_norm_b'], C['hh64']))

    kv = _norm_matmul(mem, W['norm_mem_g'], W['xattn_wkv'])
    x = _xattn(x, kv, W['norm_x_g'], W['xattn_wq'], W['xattn_wo'], bsz, t, n_mem)

    q, h = _norm_matmul(x, W['norm_ffn_g'], W['peer_wq'], emit_h=True)
    s1, s2, st = _peer_topk(q, W['peer_sk1'], W['peer_sk2'])
    return _peer_experts(h, s1, s2, st, x, W['peer_u'], W['peer_vt'])


def _run_trunk(x, mem, P, weights, C):
    bsz, t, d = x.shape
    n_mem = mem.shape[1]
    x = x.reshape(bsz * t, d)
    mem = mem.reshape(bsz * n_mem, d)
    for W in weights:
        x = _encoder_layer(x, mem, W, C, bsz, t, n_mem)
    return _final_norm(x, P['final_norm_g']).reshape(bsz, t, d)


def _hgrn_lower_bounds(logits):
    sm = jax.nn.softmax(logits.astype(F32), axis=0)
    return jnp.cumsum(sm, axis=0) - sm[0]


def kernel(x_prompt, x_sample, mem_prompt, mem_sample, norm_mix_g, w_in, gla_gate_up_f, gla_gate_up_b, gla_gate_bias_f, gla_gate_bias_b, gla_norm_g, hgrn_lb_logits, hgrn_norm_g, rwkv_mu_f, rwkv_mu_b, rwkv_w0_f, rwkv_w2_f, rwkv_w0_b, rwkv_w2_b, rwkv_a0, rwkv_a2, rwkv_g2, rwkv_k_k, rwkv_k_a, rwkv_r_k, rwkv_norm_g, rwkv_norm_b, w_branch_gla, w_branch_hgrn, w_branch_rwkv, w_out, norm_x_g, norm_mem_g, xattn_wq, xattn_wk, xattn_wv, xattn_wo, norm_ffn_g, peer_wq, peer_subkeys_1, peer_subkeys_2, peer_u, peer_v, final_norm_g):
    P = dict(norm_mix_g=norm_mix_g, w_in=w_in, gla_gate_up_f=gla_gate_up_f, gla_gate_up_b=gla_gate_up_b,
             gla_gate_bias_f=gla_gate_bias_f, gla_gate_bias_b=gla_gate_bias_b, gla_norm_g=gla_norm_g,
             hgrn_lb_logits=hgrn_lb_logits, hgrn_norm_g=hgrn_norm_g, rwkv_mu_f=rwkv_mu_f, rwkv_mu_b=rwkv_mu_b,
             rwkv_w0_f=rwkv_w0_f, rwkv_w2_f=rwkv_w2_f, rwkv_w0_b=rwkv_w0_b, rwkv_w2_b=rwkv_w2_b,
             rwkv_a0=rwkv_a0, rwkv_a2=rwkv_a2, rwkv_g2=rwkv_g2, rwkv_k_k=rwkv_k_k, rwkv_k_a=rwkv_k_a,
             rwkv_r_k=rwkv_r_k, rwkv_norm_g=rwkv_norm_g, rwkv_norm_b=rwkv_norm_b, w_branch_gla=w_branch_gla,
             w_branch_hgrn=w_branch_hgrn, w_branch_rwkv=w_branch_rwkv, w_out=w_out, norm_x_g=norm_x_g,
             norm_mem_g=norm_mem_g, xattn_wq=xattn_wq, xattn_wk=xattn_wk, xattn_wv=xattn_wv, xattn_wo=xattn_wo,
             norm_ffn_g=norm_ffn_g, peer_wq=peer_wq, peer_subkeys_1=peer_subkeys_1,
             peer_subkeys_2=peer_subkeys_2, peer_u=peer_u, peer_v=peer_v, final_norm_g=final_norm_g)
    depth = w_in.shape[0]
    lbs = _hgrn_lower_bounds(hgrn_lb_logits)
    weights = [_layer_weights(P, l, lbs[l]) for l in range(depth)]
    C = _shared_consts()
    return (_run_trunk(x_prompt, mem_prompt, P, weights, C),
            _run_trunk(x_sample, mem_sample, P, weights, C))
```
